```python
import jax, jax.numpy as jnp
from jax import lax
import numpy as np

D_MODEL = 1024
BATCH = 2
SEQ = 8192
DEPTH = 1
DEC_BATCH = 128
DEC_SEQ = 1
PAST_LEN = 16384
PAGE_SIZE = 128

HEAD_DIM = 64
N_HEADS_A = 8
N_HEADS_B = 8
N_KV_B = 2
GROUP_B = N_HEADS_B // N_KV_B
DILATED = ((128, 1), (512, 4), (2048, 16))
WIN_A = 2048
WIN_B = 128
QBLOCK = 128
D_FF = 2816
D_PLE = 256
ROPE_THETA = 10000.0
EPS = 1e-6
SCALE = HEAD_DIM ** -0.5
QA_W = N_HEADS_A * HEAD_DIM
QB_W = N_HEADS_B * HEAD_DIM
KVB_W = N_KV_B * HEAD_DIM
MIX_W = QA_W + QB_W
IN_W = 3 * QA_W + QB_W + 2 * KVB_W
SPLITS = (QA_W, 2 * QA_W, 3 * QA_W, 3 * QA_W + QB_W, 3 * QA_W + QB_W + KVB_W)

kernel_name = 'hybrid_dilated_swa_sink_decoder_step'


def _rmsnorm(x, g):
    xf = x.astype(jnp.float32)
    y = xf * lax.rsqrt(jnp.mean(xf * xf, axis=-1, keepdims=True) + EPS)
    return (y * g.astype(jnp.float32)).astype(x.dtype)


def _rope(x, pos):
    half = HEAD_DIM // 2
    inv = jnp.power(ROPE_THETA, -jnp.arange(half, dtype=jnp.float32) * 2.0 / HEAD_DIM)
    ang = pos.astype(jnp.float32)[:, None] * inv[None, :]
    c = jnp.cos(ang)[:, None, :]
    s = jnp.sin(ang)[:, None, :]
    x1 = x[..., :half].astype(jnp.float32)
    x2 = x[..., half:].astype(jnp.float32)
    return jnp.concatenate([x1 * c - x2 * s, x2 * c + x1 * s], axis=-1).astype(x.dtype)


def _ffn_half(h, g_pre, g_post, w_gate, w_up, w_down):
    u = _rmsnorm(h, g_pre)
    y = (jax.nn.silu(u @ w_gate) * (u @ w_up)) @ w_down
    return h + 0.5 * _rmsnorm(y, g_post)


def _project(h, g_pre, w_in, pos):
    b, t = h.shape[:2]
    z = _rmsnorm(h, g_pre) @ w_in
    qa, ka, va, qb, kb, vb = jnp.split(z, SPLITS, axis=-1)
    qa = _rope(qa.reshape(b, t, N_HEADS_A, HEAD_DIM), pos)
    ka = _rope(ka.reshape(b, t, N_HEADS_A, HEAD_DIM), pos)
    va = va.reshape(b, t, N_HEADS_A, HEAD_DIM)
    qb = _rope(qb.reshape(b, t, N_HEADS_B, HEAD_DIM), pos)
    kb = _rope(kb.reshape(b, t, N_KV_B, HEAD_DIM), pos)
    vb = vb.reshape(b, t, N_KV_B, HEAD_DIM)
    return qa, ka, va, qb, kb, vb


def _dilated_mix(q, k, v, qidx):
    outs, lses = [], []
    for window, dil in DILATED:
        offs = dil * jnp.arange(window // dil + 1, dtype=jnp.int32)
        idx = qidx[:, None] - offs[None, :]
        valid = idx >= 0
        idx = jnp.maximum(idx, 0)
        kg = k[:, idx]
        vg = v[:, idx]
        s = jnp.einsum('bthd,btmhd->bhtm', q, kg, preferred_element_type=jnp.float32) * SCALE
        s = jnp.where(valid, s, -jnp.inf)
        mx = jnp.max(s, axis=-1, keepdims=True)
        e = jnp.exp(s - mx)
        den = jnp.sum(e, axis=-1, keepdims=True)
        outs.append(jnp.einsum('bhtm,btmhd->bthd', e / den, vg.astype(jnp.float32)))
        lses.append((mx + jnp.log(den))[..., 0])
    wts = jax.nn.softmax(jnp.stack(lses, axis=0), axis=0)
    wts = jnp.transpose(wts, (0, 1, 3, 2))[..., None]
    out = jnp.sum(wts * jnp.stack(outs, axis=0), axis=0)
    return out.astype(q.dtype)


def _dilated_prompt(q, k, v):
    b, s, h, d = q.shape
    nb = s // QBLOCK

    def block(i):
        q_blk = lax.dynamic_slice_in_dim(q, i * QBLOCK, QBLOCK, axis=1)
        qidx = i * QBLOCK + jnp.arange(QBLOCK, dtype=jnp.int32)
        return _dilated_mix(q_blk, k, v, qidx)

    out = lax.map(block, jnp.arange(nb, dtype=jnp.int32))
    return jnp.moveaxis(out, 0, 1).reshape(b, s, h, d)


def _sink_probs(s, sink, valid):
    s = jnp.where(valid, s, -jnp.inf)
    mx = jnp.maximum(jnp.max(s, axis=-1, keepdims=True), sink)
    e = jnp.exp(s - mx)
    den = jnp.sum(e, axis=-1, keepdims=True) + jnp.exp(sink - mx)
    return e / den


def _swa_prompt(q, k, v, sink):
    b, s = q.shape[:2]
    nb = s // QBLOCK
    qg = q.reshape(b, nb, QBLOCK, N_KV_B, GROUP_B, HEAD_DIM)
    kb = k.reshape(b, nb, QBLOCK, N_KV_B, HEAD_DIM)
    vb = v.reshape(b, nb, QBLOCK, N_KV_B, HEAD_DIM)
    padw = ((0, 0), (1, 0), (0, 0), (0, 0), (0, 0))
    kk = jnp.concatenate([jnp.pad(kb, padw)[:, :-1], kb], axis=2)
    vv = jnp.concatenate([jnp.pad(vb, padw)[:, :-1], vb], axis=2)
    qi = jnp.arange(QBLOCK, dtype=jnp.int32)[:, None]
    kj = jnp.arange(2 * QBLOCK, dtype=jnp.int32)[None, :]
    dist = QBLOCK + qi - kj
    kpos = (jnp.arange(nb, dtype=jnp.int32) * QBLOCK)[:, None, None] - QBLOCK + kj[None]
    valid = (dist >= 0) & (dist <= WIN_B) & (kpos >= 0)
    sc = jnp.einsum('bnqkgd,bnmkd->bkgnqm', qg, kk, preferred_element_type=jnp.float32) * SCALE
    p = _sink_probs(sc, sink[None, :, :, None, None, None], valid)
    o = jnp.einsum('bkgnqm,bnmkd->bnqkgd', p, vv.astype(jnp.float32))
    return o.reshape(b, s, QB_W).astype(q.dtype)


def _swa_sample(q, kk, vv, sink, qpos, kpos):
    b, t = q.shape[:2]
    qg = q.reshape(b, t, N_KV_B, GROUP_B, HEAD_DIM)
    sc = jnp.einsum('btkgd,blkd->bkgtl', qg, kk, preferred_element_type=jnp.float32) * SCALE
    dist = qpos[:, None] - kpos[None, :]
    valid = (dist >= 0) & (dist <= WIN_B)
    p = _sink_probs(sc, sink[None, :, :, None, None], valid)
    o = jnp.einsum('bkgtl,blkd->btkgd', p, vv.astype(jnp.float32))
    return o.reshape(b, t, QB_W).astype(q.dtype)


def _mix_out(h, oa, ob, w_out, g_post):
    b, t = h.shape[:2]
    y = jnp.concatenate([oa.reshape(b, t, QA_W), ob], axis=-1) @ w_out
    return h + _rmsnorm(y, g_post)


def _ple(h, p, g_pre, g_post, w_gate, w_proj):
    u = _rmsnorm(h, g_pre)
    y = jax.nn.sigmoid(u @ w_gate) * (p @ w_proj)
    return h + _rmsnorm(y, g_post)


def setup_inputs(seed: int = 0) -> dict:
    key = jax.random.key(seed)
    ks = jax.random.split(key, 32)
    f32 = jnp.float32
    la = min(WIN_A, PAST_LEN)
    lb = min(WIN_B, PAST_LEN)

    def nrm(k, shape, scale):
        return jax.random.normal(k, shape, f32) * scale

    def gain(k):
        return 1.0 + nrm(k, (DEPTH, D_MODEL), 0.02)

    return {
        'x_prompt': nrm(ks[0], (BATCH, SEQ, D_MODEL), 1.0),
        'x_sample': nrm(ks[1], (DEC_BATCH, DEC_SEQ, D_MODEL), 1.0),
        'cache_a_k': nrm(ks[2], (DEPTH, DEC_BATCH, la, N_HEADS_A, HEAD_DIM), 1.0),
        'cache_a_v': nrm(ks[3], (DEPTH, DEC_BATCH, la, N_HEADS_A, HEAD_DIM), 1.0),
        'cache_b_k': nrm(ks[4], (DEPTH, DEC_BATCH, lb, N_KV_B, HEAD_DIM), 1.0),
        'cache_b_v': nrm(ks[5], (DEPTH, DEC_BATCH, lb, N_KV_B, HEAD_DIM), 1.0),
        'p_prompt': nrm(ks[6], (DEPTH, BATCH, SEQ, D_PLE), 1.0),
        'p_sample': nrm(ks[7], (DEPTH, DEC_BATCH, DEC_SEQ, D_PLE), 1.0),
        'norm_f1_pre': gain(ks[8]),
        'norm_f1_post': gain(ks[9]),
        'w_f1_gate': nrm(ks[10], (DEPTH, D_MODEL, D_FF), D_MODEL ** -0.5),
        'w_f1_up': nrm(ks[11], (DEPTH, D_MODEL, D_FF), D_MODEL ** -0.5),
        'w_f1_down': nrm(ks[12], (DEPTH, D_FF, D_MODEL), D_FF ** -0.5),
        'norm_mix_pre': gain(ks[13]),
        'norm_mix_post': gain(ks[14]),
        'w_in': nrm(ks[15], (DEPTH, D_MODEL, IN_W), D_MODEL ** -0.5),
        'sinks_b': nrm(ks[16], (DEPTH, N_HEADS_B), 0.5),
        'w_out': nrm(ks[17], (DEPTH, MIX_W, D_MODEL), MIX_W ** -0.5),
        'norm_f2_pre': gain(ks[18]),
        'norm_f2_post': gain(ks[19]),
        'w_f2_gate': nrm(ks[20], (DEPTH, D_MODEL, D_FF), D_MODEL ** -0.5),
        'w_f2_up': nrm(ks[21], (DEPTH, D_MODEL, D_FF), D_MODEL ** -0.5),
        'w_f2_down': nrm(ks[22], (DEPTH, D_FF, D_MODEL), D_FF ** -0.5),
        'norm_ple_pre': gain(ks[23]),
        'norm_ple_post': gain(ks[24]),
        'w_ple_gate': nrm(ks[25], (DEPTH, D_MODEL, D_MODEL), D_MODEL ** -0.5),
        'w_ple_proj': nrm(ks[26], (DEPTH, D_PLE, D_MODEL), D_PLE ** -0.5),
    }


def reference(x_prompt, x_sample, cache_a_k, cache_a_v, cache_b_k, cache_b_v, p_prompt, p_sample,
              norm_f1_pre, norm_f1_post, w_f1_gate, w_f1_up, w_f1_down,
              norm_mix_pre, norm_mix_post, w_in, sinks_b, w_out,
              norm_f2_pre, norm_f2_post, w_f2_gate, w_f2_up, w_f2_down,
              norm_ple_pre, norm_ple_post, w_ple_gate, w_ple_proj):
    pos_p = jnp.arange(SEQ, dtype=jnp.int32)
    pos_s = PAST_LEN + jnp.arange(DEC_SEQ, dtype=jnp.int32)
    la = cache_a_k.shape[2]
    lb = cache_b_k.shape[2]
    keep_a = min(WIN_A, SEQ)
    keep_b = min(WIN_B, SEQ)
    kpos_b = PAST_LEN - lb + jnp.arange(lb + DEC_SEQ, dtype=jnp.int32)
    qidx_a = la + jnp.arange(DEC_SEQ, dtype=jnp.int32)
    hp, hs = x_prompt, x_sample
    nak_p, nav_p, nbk_p, nbv_p = [], [], [], []
    nak_s, nav_s, nbk_s, nbv_s = [], [], [], []
    for i in range(DEPTH):
        hp = _ffn_half(hp, norm_f1_pre[i], norm_f1_post[i], w_f1_gate[i], w_f1_up[i], w_f1_down[i])
        hs = _ffn_half(hs, norm_f1_pre[i], norm_f1_post[i], w_f1_gate[i], w_f1_up[i], w_f1_down[i])
        sink = sinks_b[i].astype(jnp.float32).reshape(N_KV_B, GROUP_B)
        qa, ka, va, qb, kb, vb = _project(hp, norm_mix_pre[i], w_in[i], pos_p)
        oa = _dilated_prompt(qa, ka, va)
        ob = _swa_prompt(qb, kb, vb, sink)
        hp = _mix_out(hp, oa, ob, w_out[i], norm_mix_post[i])
        nak_p.append(ka[:, SEQ - keep_a:])
        nav_p.append(va[:, SEQ - keep_a:])
        nbk_p.append(kb[:, SEQ - keep_b:])
        nbv_p.append(vb[:, SEQ - keep_b:])
        qa_s, ka_s, va_s, qb_s, kb_s, vb_s = _project(hs, norm_mix_pre[i], w_in[i], pos_s)
        kka = jnp.concatenate([cache_a_k[i].astype(ka_s.dtype), ka_s], axis=1)
        vva = jnp.concatenate([cache_a_v[i].astype(va_s.dtype), va_s], axis=1)
        kkb = jnp.concatenate([cache_b_k[i].astype(kb_s.dtype), kb_s], axis=1)
        vvb = jnp.concatenate([cache_b_v[i].astype(vb_s.dtype), vb_s], axis=1)
        oa_s = _dilated_mix(qa_s, kka, vva, qidx_a)
        ob_s = _swa_sample(qb_s, kkb, vvb, sink, pos_s, kpos_b)
        hs = _mix_out(hs, oa_s, ob_s, w_out[i], norm_mix_post[i])
        nak_s.append(kka[:, DEC_SEQ:])
        nav_s.append(vva[:, DEC_SEQ:])
        nbk_s.append(kkb[:, DEC_SEQ:])
        nbv_s.append(vvb[:, DEC_SEQ:])
        hp = _ffn_half(hp, norm_f2_pre[i], norm_f2_post[i], w_f2_gate[i], w_f2_up[i], w_f2_down[i])
        hs = _ffn_half(hs, norm_f2_pre[i], norm_f2_post[i], w_f2_gate[i], w_f2_up[i], w_f2_down[i])
        hp = _ple(hp, p_prompt[i], norm_ple_pre[i], norm_ple_post[i], w_ple_gate[i], w_ple_proj[i])
        hs = _ple(hs, p_sample[i], norm_ple_pre[i], norm_ple_post[i], w_ple_gate[i], w_ple_proj[i])
    return (hp, hs,
            jnp.stack(nak_p), jnp.stack(nav_p), jnp.stack(nbk_p), jnp.stack(nbv_p),
            jnp.stack(nak_s), jnp.stack(nav_s), jnp.stack(nbk_s), jnp.stack(nbv_s))
```

```python
import functools

import jax
import jax.numpy as jnp
from jax import lax
from jax.experimental import pallas as pl
from jax.experimental.pallas import tpu as pltpu

F32 = jnp.float32
BF16 = jnp.bfloat16

D_MODEL = 1024
HEAD_DIM = 64
HALF = HEAD_DIM // 2
N_HEADS_A = 8
N_HEADS_B = 8
N_KV_B = 2
GROUP_B = N_HEADS_B // N_KV_B
DILATIONS = (1, 4, 16)
WIN_STEPS = 128
WIN_A = 2048
WIN_B = 128
PAST_LEN = 16384
D_FF = 2816
D_PLE = 256
ROPE_THETA = 10000.0
EPS = 1e-6
SCALE = HEAD_DIM ** -0.5
QA_W = N_HEADS_A * HEAD_DIM
QB_W = N_HEADS_B * HEAD_DIM
KVB_W = N_KV_B * HEAD_DIM
IN_W = 3 * QA_W + QB_W + 2 * KVB_W
LANES = 128
QBLK = 128

ROW_TILE = 256
ATT_ROWS = 512
VMEM_LIMIT = 56 * 1024 * 1024


def _whole(shape):
    nd = len(shape)
    return pl.BlockSpec(shape, lambda *_: (0,) * nd, pipeline_mode=pl.Buffered(1))


def _rmsnorm(x, g):
    return x * lax.rsqrt(jnp.mean(x * x, axis=-1, keepdims=True) + EPS) * g


def _swiglu_half_step(x, g_pre, g_post, wg_ref, wu_ref, wd_ref):
    u = _rmsnorm(x, g_pre).astype(BF16)
    gate = jnp.dot(u, wg_ref[...], preferred_element_type=F32)
    up = jnp.dot(u, wu_ref[...], preferred_element_type=F32)
    act = (gate * jax.nn.sigmoid(gate) * up).astype(BF16)
    y = jnp.dot(act, wd_ref[...], preferred_element_type=F32)
    return x + 0.5 * _rmsnorm(y, g_post)


def _rope(z, cos, sin_signed):
    rows, width = z.shape
    lane = lax.broadcasted_iota(jnp.int32, (rows, LANES), 1)
    first_half = (lane & HALF) == 0
    outs = []
    for c in range(width // LANES):
        zc = z[:, c * LANES:(c + 1) * LANES]
        partner = jnp.where(first_half,
                            pltpu.roll(zc, LANES - HALF, 1),
                            pltpu.roll(zc, HALF, 1))
        outs.append(zc * cos + partner * sin_signed)
    return outs[0] if len(outs) == 1 else jnp.concatenate(outs, axis=1)


def _project(h, g_pre, win_ref, cos, sin_signed):
    u = _rmsnorm(h, g_pre).astype(BF16)

    def cols(lo, hi):
        return jnp.dot(u, win_ref[:, lo:hi], preferred_element_type=F32)

    o = 0
    qa = _rope(cols(o, o + QA_W), cos, sin_signed) * SCALE
    o += QA_W
    ka = _rope(cols(o, o + QA_W), cos, sin_signed)
    o += QA_W
    va = cols(o, o + QA_W)
    o += QA_W
    qb = _rope(cols(o, o + QB_W), cos, sin_signed) * SCALE
    o += QB_W
    kb = _rope(cols(o, o + KVB_W), cos, sin_signed)
    o += KVB_W
    vb = cols(o, o + KVB_W)
    return qa, ka, va, qb, kb, vb


def _dup_heads(x):
    lane = lax.broadcasted_iota(jnp.int32, x.shape, 1)
    swapped = pltpu.roll(x, HEAD_DIM, 1)
    lo = lane < HEAD_DIM
    return jnp.where(lo, x, swapped), jnp.where(lo, swapped, x)


def _stage1_prompt_kernel(x_ref, cos_ref, sin_ref, g1a_ref, g1b_ref, gmix_ref,
                          wg_ref, wu_ref, wd_ref, win_ref,
                          h_ref, qa1_ref, ka1_ref, va1_ref, qa4_ref, ka4_ref, va4_ref,
                          qa16_ref, ka16_ref, va16_ref, qb_ref, kb2_ref, vb2_ref,
                          nak_ref, nav_ref, nbk_ref, nbv_ref, stage_ref):
    tm = x_ref.shape[0]
    h = _swiglu_half_step(x_ref[...], g1a_ref[...], g1b_ref[...], wg_ref, wu_ref, wd_ref)
    h_ref[...] = h
    qa, ka, va, qb, kb, vb = _project(h, gmix_ref[...], win_ref, cos_ref[...], sin_ref[...])

    def emit(z, nat_ref, dec_refs):
        nat_ref[...] = z.astype(BF16)
        for c in range(QA_W // LANES):
            stage_ref[c] = z[:, c * LANES:(c + 1) * LANES]
        for d, ref in dec_refs:
            n = tm // d
            for r in range(d):
                for c in range(QA_W // LANES):
                    lo = r * QA_W + c * LANES
                    ref[:, lo:lo + LANES] = stage_ref[c, pl.ds(r, n, stride=d), :].astype(BF16)

    emit(qa, qa1_ref, ((4, qa4_ref), (16, qa16_ref)))
    emit(ka, ka1_ref, ((4, ka4_ref), (16, ka16_ref)))
    emit(va, va1_ref, ((4, va4_ref), (16, va16_ref)))
    qb_ref[...] = qb.astype(BF16)
    k0, k1 = _dup_heads(kb)
    kb2_ref[:, 0:LANES] = k0.astype(BF16)
    kb2_ref[:, LANES:2 * LANES] = k1.astype(BF16)
    v0, v1 = _dup_heads(vb)
    vb2_ref[:, 0:LANES] = v0.astype(BF16)
    vb2_ref[:, LANES:2 * LANES] = v1.astype(BF16)
    nak_ref[...] = ka
    nav_ref[...] = va
    nbk_ref[...] = kb[tm - WIN_B:, :]
    nbv_ref[...] = vb[tm - WIN_B:, :]


def _stage1_sample_kernel(x_ref, cos_ref, sin_ref, g1a_ref, g1b_ref, gmix_ref,
                          wg_ref, wu_ref, wd_ref, win_ref,
                          h_ref, qa_ref, ka_ref, va_ref, qb_ref, kb_ref, vb_ref):
    h = _swiglu_half_step(x_ref[...], g1a_ref[...], g1b_ref[...], wg_ref, wu_ref, wd_ref)
    h_ref[...] = h
    qa, ka, va, qb, kb, vb = _project(h, gmix_ref[...], win_ref, cos_ref[...], sin_ref[...])
    qa_ref[...] = qa
    ka_ref[...] = ka
    va_ref[...] = va
    qb_ref[...] = qb
    kb_ref[...] = kb
    vb_ref[...] = vb


def _stage1_weight_specs():
    return [_whole((1, D_MODEL))] * 3 + [
        _whole((D_MODEL, D_FF)), _whole((D_MODEL, D_FF)), _whole((D_FF, D_MODEL)),
        _whole((D_MODEL, IN_W))]


def _stage1_prompt(x, cos, sin, g1a, g1b, gmix, wg, wu, wd, win):
    nb, seq, _ = x.shape
    tm = ROW_TILE
    nt = seq // tm
    first_kept = (seq - WIN_A) // tm

    def rows(width, dtype, d=1):
        shape = jax.ShapeDtypeStruct((nb, seq // d, d * width), dtype)
        spec = pl.BlockSpec((None, tm // d, d * width), lambda b, i: (b, i, 0))
        return shape, spec

    outs = [rows(D_MODEL, F32)]
    outs += [rows(QA_W, BF16)] * 3 + [rows(QA_W, BF16, 4)] * 3 + [rows(QA_W, BF16, 16)] * 3
    outs += [rows(QB_W, BF16), rows(2 * LANES, BF16), rows(2 * LANES, BF16)]
    kept = (jax.ShapeDtypeStruct((nb, WIN_A, QA_W), F32),
            pl.BlockSpec((None, tm, QA_W), lambda b, i: (b, jnp.maximum(i - first_kept, 0), 0)))
    last = (jax.ShapeDtypeStruct((nb, WIN_B, KVB_W), F32),
            pl.BlockSpec((None, WIN_B, KVB_W), lambda b, i: (b, 0, 0)))
    outs += [kept, kept, last, last]
    in_specs = [pl.BlockSpec((None, tm, D_MODEL), lambda b, i: (b, i, 0)),
                pl.BlockSpec((tm, LANES), lambda b, i: (i, 0)),
                pl.BlockSpec((tm, LANES), lambda b, i: (i, 0))] + _stage1_weight_specs()
    return pl.pallas_call(
        _stage1_prompt_kernel,
        out_shape=[o[0] for o in outs],
        grid=(nb, nt),
        in_specs=in_specs,
        out_specs=[o[1] for o in outs],
        scratch_shapes=[pltpu.VMEM((QA_W // LANES, tm, LANES), F32)],
        compiler_params=pltpu.CompilerParams(
            dimension_semantics=("arbitrary", "arbitrary"), vmem_limit_bytes=VMEM_LIMIT),
        name="stage1_prompt",
    )(x, cos, sin, g1a, g1b, gmix, wg, wu, wd, win)


def _stage1_sample(x, cos, sin, g1a, g1b, gmix, wg, wu, wd, win):
    n = x.shape[0]
    widths = (D_MODEL, QA_W, QA_W, QA_W, QB_W, KVB_W, KVB_W)
    return pl.pallas_call(
        _stage1_sample_kernel,
        out_shape=[jax.ShapeDtypeStruct((n, w), F32) for w in widths],
        grid=(1,),
        in_specs=[_whole((n, D_MODEL)), _whole((n, LANES)), _whole((n, LANES))]
        + _stage1_weight_specs(),
        out_specs=[pl.BlockSpec((n, w), lambda i: (0, 0)) for w in widths],
        compiler_params=pltpu.CompilerParams(
            dimension_semantics=("arbitrary",), vmem_limit_bytes=VMEM_LIMIT),
        name="stage1_sample",
    )(x, cos, sin, g1a, g1b, gmix, wg, wu, wd, win)


def _banded_attention_kernel(*refs, with_sink, with_lse):
    refs = list(refs)
    sink_ref = refs.pop(0) if with_sink else None
    q_ref, kp_ref, kc_ref, vp_ref, vc_ref, o_ref = refs[:6]
    lse_ref = refs[6] if with_lse else None
    rows = q_ref.shape[0]
    col_block = pl.program_id(1)
    has_prev = pl.program_id(2) > 0

    row = lax.broadcasted_iota(jnp.int32, (QBLK, QBLK), 0)
    col = lax.broadcasted_iota(jnp.int32, (QBLK, QBLK), 1)
    in_band_prev = col >= row
    in_band_cur = col <= row
    low_lanes = lax.broadcasted_iota(jnp.int32, (QBLK, LANES), 1) < HEAD_DIM
    nt_dims = (((1,), (1,)), ((), ()))

    for jb in range(rows // QBLK):
        cur = slice(jb * QBLK, (jb + 1) * QBLK)
        q = q_ref[cur, :]
        if jb == 0:
            k_prev, v_prev = kp_ref[...], vp_ref[...]
            mask_prev = jnp.logical_and(in_band_prev, has_prev)
        else:
            prev = slice((jb - 1) * QBLK, jb * QBLK)
            k_prev, v_prev = kc_ref[prev, :], vc_ref[prev, :]
            mask_prev = in_band_prev
        k_cur, v_cur = kc_ref[cur, :], vc_ref[cur, :]
        per_head = []
        for hh in range(2):
            qh = jnp.where(low_lanes if hh == 0 else jnp.logical_not(low_lanes), q, jnp.zeros_like(q))
            s_prev = lax.dot_general(qh, k_prev, nt_dims, preferred_element_type=F32)
            s_cur = lax.dot_general(qh, k_cur, nt_dims, preferred_element_type=F32)
            s_prev = jnp.where(mask_prev, s_prev, -jnp.inf)
            s_cur = jnp.where(in_band_cur, s_cur, -jnp.inf)
            m = jnp.maximum(jnp.max(s_prev, axis=1, keepdims=True),
                            jnp.max(s_cur, axis=1, keepdims=True))
            if with_sink:
                sink = sink_ref[2 * col_block + hh]
                m = jnp.maximum(m, sink)
            e_prev = jnp.exp(s_prev - m)
            e_cur = jnp.exp(s_cur - m)
            den = jnp.sum(e_prev, axis=1, keepdims=True) + jnp.sum(e_cur, axis=1, keepdims=True)
            if with_sink:
                den = den + jnp.exp(sink - m)
            acc = (jnp.dot(e_prev.astype(BF16), v_prev, preferred_element_type=F32)
                   + jnp.dot(e_cur.astype(BF16), v_cur, preferred_element_type=F32))
            per_head.append((acc / den, m + jnp.log(den)))
        o_ref[cur, :] = jnp.where(low_lanes, per_head[0][0], per_head[1][0]).astype(o_ref.dtype)
        if with_lse:
            lse_ref[cur, :] = jnp.where(low_lanes, per_head[0][1], per_head[1][1])


def _banded_attention(q, k, v, *, k_col, sinks=None, with_lse, name):
    nb, length, width = q.shape
    rows = min(ATT_ROWS, length)
    sub = rows // QBLK
    grid = (nb, width // LANES, length // rows)
    q_spec = pl.BlockSpec((None, rows, LANES), lambda b, c, i: (b, i, c))
    cur_spec = pl.BlockSpec((None, rows, LANES), lambda b, c, i: (b, i, k_col(c)))
    prev_spec = pl.BlockSpec((None, QBLK, LANES),
                             lambda b, c, i: (b, jnp.maximum(i * sub - 1, 0), k_col(c)))
    in_specs = [q_spec, prev_spec, cur_spec, prev_spec, cur_spec]
    args = [q, k, k, v, v]
    if sinks is not None:
        in_specs = [pl.BlockSpec(memory_space=pltpu.SMEM)] + in_specs
        args = [sinks] + args
    out_shape = [jax.ShapeDtypeStruct(q.shape, BF16)]
    out_specs = [q_spec]
    if with_lse:
        out_shape.append(jax.ShapeDtypeStruct(q.shape, F32))
        out_specs.append(q_spec)
    return pl.pallas_call(
        functools.partial(_banded_attention_kernel, with_sink=sinks is not None, with_lse=with_lse),
        out_shape=out_shape,
        grid=grid,
        in_specs=in_specs,
        out_specs=out_specs,
        compiler_params=pltpu.CompilerParams(
            dimension_semantics=("arbitrary",) * 3, vmem_limit_bytes=VMEM_LIMIT),
        name=name,
    )(*args)


def _slide_window(buf_t, new_col):
    length = buf_t.shape[1]
    lane = lax.broadcasted_iota(jnp.int32, buf_t.shape, 1)
    return jnp.where(lane == length - 1, new_col, pltpu.roll(buf_t, length - 1, 1))


def _sample_attention_kernel(sink_ref, qa_ref, kan_ref, van_ref, ck_ref, cv_ref,
                             qb_ref, kbn_ref, vbn_ref, cbk_ref, cbv_ref,
                             oa_ref, ob_ref, nak_ref, nav_ref, nbk_ref, nbv_ref):
    kv_head = pl.program_id(1)
    la = ck_ref.shape[-1]
    dist = la - lax.broadcasted_iota(jnp.int32, (1, la), 1)
    count = jnp.zeros((1, la), F32)
    for d in DILATIONS:
        in_pattern = jnp.logical_and((dist & (d - 1)) == 0, dist <= d * WIN_STEPS)
        count = count + in_pattern.astype(F32)
    for h in range(ck_ref.shape[0]):
        q = qa_ref[h]
        k_t, v_t = ck_ref[h], cv_ref[h]
        k_new, v_new = kan_ref[h], van_ref[h]
        s = jnp.sum(q * k_t, axis=0, keepdims=True)
        s = jnp.where(count > 0.0, s, -jnp.inf)
        s_new = jnp.sum(q * k_new, axis=0, keepdims=True)
        m = jnp.maximum(jnp.max(s, axis=1, keepdims=True), s_new)
        e = jnp.exp(s - m) * count
        e_new = float(len(DILATIONS)) * jnp.exp(s_new - m)
        den = jnp.sum(e, axis=1, keepdims=True) + e_new
        acc = jnp.sum(v_t * e, axis=1, keepdims=True) + e_new * v_new
        oa_ref[h] = acc / den
        nak_ref[h] = _slide_window(k_t, k_new)
        nav_ref[h] = _slide_window(v_t, v_new)

    k_t, v_t = cbk_ref[...], cbv_ref[...]
    k_new, v_new = kbn_ref[...], vbn_ref[...]
    for g in range(GROUP_B):
        q = qb_ref[g]
        sink = sink_ref[kv_head * GROUP_B + g]
        s = jnp.sum(q * k_t, axis=0, keepdims=True)
        s_new = jnp.sum(q * k_new, axis=0, keepdims=True)
        m = jnp.maximum(jnp.maximum(jnp.max(s, axis=1, keepdims=True), s_new), sink)
        e = jnp.exp(s - m)
        e_new = jnp.exp(s_new - m)
        den = jnp.sum(e, axis=1, keepdims=True) + e_new + jnp.exp(sink - m)
        ob_ref[g] = (jnp.sum(v_t * e, axis=1, keepdims=True) + e_new * v_new) / den
    nbk_ref[...] = _slide_window(k_t, k_new)
    nbv_ref[...] = _slide_window(v_t, v_new)


def _sample_attention(sinks, qa, ka, va, cak_t, cav_t, qb, kb, vb, cbk_t, cbv_t):
    n, _, _, la = cak_t.shape
    lb = cbk_t.shape[-1]
    heads = N_HEADS_A // N_KV_B

    def heads_spec(minor):
        return pl.BlockSpec((None, heads, HEAD_DIM, minor), lambda b, g: (b, g, 0, 0))

    def kv_spec(minor):
        return pl.BlockSpec((None, None, HEAD_DIM, minor), lambda b, g: (b, g, 0, 0))

    in_specs = [pl.BlockSpec(memory_space=pltpu.SMEM),
                heads_spec(1), heads_spec(1), heads_spec(1), heads_spec(la), heads_spec(la),
                heads_spec(1), kv_spec(1), kv_spec(1), kv_spec(lb), kv_spec(lb)]
    out_shape = [jax.ShapeDtypeStruct(qa.shape, F32), jax.ShapeDtypeStruct(qb.shape, F32),
                 jax.ShapeDtypeStruct(cak_t.shape, F32), jax.ShapeDtypeStruct(cav_t.shape, F32),
                 jax.ShapeDtypeStruct(cbk_t.shape, F32), jax.ShapeDtypeStruct(cbv_t.shape, F32)]
    out_specs = [heads_spec(1), heads_spec(1), heads_spec(la), heads_spec(la), kv_spec(lb), kv_spec(lb)]
    return pl.pallas_call(
        _sample_attention_kernel,
        out_shape=out_shape,
        grid=(n, N_KV_B),
        in_specs=in_specs,
        out_specs=out_specs,
        compiler_params=pltpu.CompilerParams(
            dimension_semantics=("arbitrary", "arbitrary"), vmem_limit_bytes=VMEM_LIMIT),
        name="sample_attention",
    )(sinks, qa, ka, va, cak_t, cav_t, qb, kb, vb, cbk_t, cbv_t)


def _stage3_tail(h, oa, ob, p, norm_refs, weight_refs, out_ref):
    gmix_ref, g2a_ref, g2b_ref, gpa_ref, gpb_ref = norm_refs
    wout_ref, wg_ref, wu_ref, wd_ref, wpg_ref, wpp_ref = weight_refs
    y = (jnp.dot(oa.astype(BF16), wout_ref[0:QA_W, :], preferred_element_type=F32)
         + jnp.dot(ob.astype(BF16), wout_ref[QA_W:QA_W + QB_W, :], preferred_element_type=F32))
    h = h + _rmsnorm(y, gmix_ref[...])
    h = _swiglu_half_step(h, g2a_ref[...], g2b_ref[...], wg_ref, wu_ref, wd_ref)
    u = _rmsnorm(h, gpa_ref[...]).astype(BF16)
    gate = jax.nn.sigmoid(jnp.dot(u, wpg_ref[...], preferred_element_type=F32))
    proj = jnp.dot(p.astype(BF16), wpp_ref[...], preferred_element_type=F32)
    out_ref[...] = h + _rmsnorm(gate * proj, gpb_ref[...])


def _stage3_prompt_kernel(h_ref, o1_ref, l1_ref, o4_ref, l4_ref, o16_ref, l16_ref, ob_ref, p_ref,
                          *rest):
    norm_refs, weight_refs, out_ref = rest[:5], rest[5:11], rest[11]
    o4s_ref, l4s_ref, o16s_ref, l16s_ref = rest[12:]
    tm = h_ref.shape[0]
    for d, src, dst in ((4, o4_ref, o4s_ref), (4, l4_ref, l4s_ref),
                        (16, o16_ref, o16s_ref), (16, l16_ref, l16s_ref)):
        n = tm // d
        for r in range(d):
            for c in range(QA_W // LANES):
                lo = r * QA_W + c * LANES
                dst[c, pl.ds(r, n, stride=d), :] = src[:, lo:lo + LANES].astype(F32)
    pieces = []
    for c in range(QA_W // LANES):
        cols = slice(c * LANES, (c + 1) * LANES)
        l1, l4, l16 = l1_ref[:, cols], l4s_ref[c], l16s_ref[c]
        m = jnp.maximum(l1, jnp.maximum(l4, l16))
        w1, w4, w16 = jnp.exp(l1 - m), jnp.exp(l4 - m), jnp.exp(l16 - m)
        pieces.append((w1 * o1_ref[:, cols].astype(F32) + w4 * o4s_ref[c] + w16 * o16s_ref[c])
                      / (w1 + w4 + w16))
    oa = jnp.concatenate(pieces, axis=1)
    _stage3_tail(h_ref[...], oa, ob_ref[...], p_ref[...], norm_refs, weight_refs, out_ref)


def _stage3_sample_kernel(h_ref, oa_ref, ob_ref, p_ref, *rest):
    norm_refs, weight_refs, out_ref = rest[:5], rest[5:11], rest[11]
    _stage3_tail(h_ref[...], oa_ref[...], ob_ref[...], p_ref[...], norm_refs, weight_refs, out_ref)


def _stage3_weight_specs():
    return [_whole((1, D_MODEL))] * 5 + [
        _whole((QA_W + QB_W, D_MODEL)), _whole((D_MODEL, D_FF)), _whole((D_MODEL, D_FF)),
        _whole((D_FF, D_MODEL)), _whole((D_MODEL, D_MODEL)), _whole((D_PLE, D_MODEL))]


def _stage3_prompt(h, o1, l1, o4, l4, o16, l16, ob, p, norms, weights):
    nb, seq, _ = h.shape
    tm = ROW_TILE

    def rows(width, d=1):
        return pl.BlockSpec((None, tm // d, d * width), lambda b, i: (b, i, 0))

    in_specs = [rows(D_MODEL), rows(QA_W), rows(QA_W), rows(QA_W, 4), rows(QA_W, 4),
                rows(QA_W, 16), rows(QA_W, 16), rows(QB_W), rows(D_PLE)] + _stage3_weight_specs()
    return pl.pallas_call(
        _stage3_prompt_kernel,
        out_shape=jax.ShapeDtypeStruct(h.shape, F32),
        grid=(nb, seq // tm),
        in_specs=in_specs,
        out_specs=rows(D_MODEL),
        scratch_shapes=[pltpu.VMEM((QA_W // LANES, tm, LANES), F32)] * 4,
        compiler_params=pltpu.CompilerParams(
            dimension_semantics=("arbitrary", "arbitrary"), vmem_limit_bytes=VMEM_LIMIT),
        name="stage3_prompt",
    )(h, o1, l1, o4, l4, o16, l16, ob, p, *norms, *weights)


def _stage3_sample(h, oa, ob, p, norms, weights):
    n = h.shape[0]
    return pl.pallas_call(
        _stage3_sample_kernel,
        out_shape=jax.ShapeDtypeStruct(h.shape, F32),
        grid=(1,),
        in_specs=[_whole((n, D_MODEL)), _whole((n, QA_W)), _whole((n, QB_W)), _whole((n, D_PLE))]
        + _stage3_weight_specs(),
        out_specs=pl.BlockSpec((n, D_MODEL), lambda i: (0, 0)),
        compiler_params=pltpu.CompilerParams(
            dimension_semantics=("arbitrary",), vmem_limit_bytes=VMEM_LIMIT),
        name="stage3_sample",
    )(h, oa, ob, p, *norms, *weights)


def _rope_tables(pos):
    inv = jnp.power(ROPE_THETA, -jnp.arange(HALF, dtype=F32) * 2.0 / HEAD_DIM)
    ang = pos.astype(F32)[:, None] * inv[None, :]
    c, s = jnp.cos(ang), jnp.sin(ang)
    return jnp.concatenate([c, c, c, c], axis=-1), jnp.concatenate([-s, s, -s, s], axis=-1)


def _layer(i, hp, hs, caches, p_prompt, p_sample, norms, weights, sinks):
    (g1a, g1b, gmix_a, gmix_b, g2a, g2b, gpa, gpb) = [g[i][None, :] for g in norms]
    (wg1, wu1, wd1, win, wout, wg2, wu2, wd2, wpg, wpp) = [w[i].astype(BF16) for w in weights]
    cak, cav, cbk, cbv = [c[i] for c in caches]
    nb, seq, _ = hp.shape
    n_dec, dec_seq, _ = hs.shape

    cos_p, sin_p = _rope_tables(jnp.arange(seq, dtype=jnp.int32))
    pos_s = jnp.broadcast_to(PAST_LEN + jnp.arange(dec_seq, dtype=jnp.int32)[None, :],
                             (n_dec, dec_seq)).reshape(-1)
    cos_s, sin_s = _rope_tables(pos_s)
    stage1_w = (g1a, g1b, gmix_a, wg1, wu1, wd1, win)
    stage3_n = (gmix_b, g2a, g2b, gpa, gpb)
    stage3_w = (wout, wg2, wu2, wd2, wpg, wpp)

    (h1, qa1, ka1, va1, qa4, ka4, va4, qa16, ka16, va16, qb, kb2, vb2,
     nak_p, nav_p, nbk_p, nbv_p) = _stage1_prompt(hp, cos_p, sin_p, *stage1_w)
    same = lambda c: c
    o1, l1 = _banded_attention(qa1, ka1, va1, k_col=same, with_lse=True, name="dilated_d1")
    o4, l4 = _banded_attention(qa4, ka4, va4, k_col=same, with_lse=True, name="dilated_d4")
    o16, l16 = _banded_attention(qa16, ka16, va16, k_col=same, with_lse=True, name="dilated_d16")
    (ob,) = _banded_attention(qb, kb2, vb2, k_col=lambda c: c // 2, sinks=sinks[i].astype(F32),
                              with_lse=False, name="swa_sink")
    hp = _stage3_prompt(h1, o1, l1, o4, l4, o16, l16, ob, p_prompt[i], stage3_n, stage3_w)

    xs = hs.reshape(n_dec * dec_seq, D_MODEL)
    h1s, qa_s, ka_s, va_s, qb_s, kb_s, vb_s = _stage1_sample(xs, cos_s, sin_s, *stage1_w)
    columns = lambda z: z.reshape(n_dec, -1, HEAD_DIM, 1)
    rows_last = lambda z: z.transpose(0, 2, 3, 1)
    rows_first = lambda z: z.transpose(0, 3, 1, 2)
    oa_s, ob_s, nak_s, nav_s, nbk_s, nbv_s = _sample_attention(
        sinks[i].astype(F32), columns(qa_s), columns(ka_s), columns(va_s),
        rows_last(cak), rows_last(cav), columns(qb_s), columns(kb_s), columns(vb_s),
        rows_last(cbk), rows_last(cbv))
    hs = _stage3_sample(h1s, oa_s.reshape(n_dec, QA_W), ob_s.reshape(n_dec, QB_W),
                        p_sample[i].reshape(n_dec * dec_seq, D_PLE), stage3_n, stage3_w)
    hs = hs.reshape(n_dec, dec_seq, D_MODEL)

    heads5 = lambda z, h: z.reshape(z.shape[0], z.shape[1], h, HEAD_DIM)
    new = (heads5(nak_p, N_HEADS_A), heads5(nav_p, N_HEADS_A), heads5(nbk_p, N_KV_B),
           heads5(nbv_p, N_KV_B), rows_first(nak_s), rows_first(nav_s), rows_first(nbk_s),
           rows_first(nbv_s))
    return hp, hs, new


def kernel(x_prompt, x_sample, cache_a_k, cache_a_v, cache_b_k, cache_b_v, p_prompt, p_sample,
           norm_f1_pre, norm_f1_post, w_f1_gate, w_f1_up, w_f1_down,
           norm_mix_pre, norm_mix_post, w_in, sinks_b, w_out,
           norm_f2_pre, norm_f2_post, w_f2_gate, w_f2_up, w_f2_down,
           norm_ple_pre, norm_ple_post, w_ple_gate, w_ple_proj):
    assert x_sample.shape[1] == 1 and cache_a_k.shape[2] == WIN_A and cache_b_k.shape[2] == WIN_B
    norms = (norm_f1_pre, norm_f1_post, norm_mix_pre, norm_mix_post,
             norm_f2_pre, norm_f2_post, norm_ple_pre, norm_ple_post)
    weights = (w_f1_gate, w_f1_up, w_f1_down, w_in, w_out,
               w_f2_gate, w_f2_up, w_f2_down, w_ple_gate, w_ple_proj)
    caches = (cache_a_k, cache_a_v, cache_b_k, cache_b_v)
    hp, hs = x_prompt, x_sample
    per_layer = []
    for i in range(norm_f1_pre.shape[0]):
        hp, hs, new = _layer(i, hp, hs, caches, p_prompt, p_sample, norms, weights, sinks_b)
        per_layer.append(new)
    stacked = [jnp.stack([layer[j] for layer in per_layer]) for j in range(8)]
    return (hp, hs, *stacked)
```

```python
import functools

import jax
import jax.numpy as jnp
from jax import lax
from jax.experimental import pallas as pl
from jax.experimental.pallas import tpu as pltpu

F32 = jnp.float32
BF16 = jnp.bfloat16

D_MODEL = 1024
HEAD_DIM = 64
HALF = HEAD_DIM // 2
N_HEADS_A = 8
N_HEADS_B = 8
N_KV_B = 2
GROUP_B = N_HEADS_B // N_KV_B
DILATIONS = (1, 4, 16)
WIN_STEPS = 128
WIN_A = 2048
WIN_B = 128
PAST_LEN = 16384
D_FF = 2816
D_PLE = 256
ROPE_THETA = 10000.0
EPS = 1e-6
SCALE = HEAD_DIM ** -0.5
QA_W = N_HEADS_A * HEAD_DIM
QB_W = N_HEADS_B * HEAD_DIM
KVB_W = N_KV_B * HEAD_DIM
IN_W = 3 * QA_W + QB_W + 2 * KVB_W
LANES = 128
QBLK = 128

ROW_TILE = 512
ROW_BLOCK = 256
ATT_ROWS = 512
VMEM_LIMIT = 56 * 1024 * 1024


def _whole(shape):
    nd = len(shape)
    return pl.BlockSpec(shape, lambda *_: (0,) * nd, pipeline_mode=pl.Buffered(1))


def _rmsnorm(x, g):
    return x * lax.rsqrt(jnp.mean(x * x, axis=-1, keepdims=True) + EPS) * g


def _mm(a, w):
    return jnp.dot(a, w, preferred_element_type=F32)


def _run_staggered(pipelines):
    depth = len(pipelines[0])
    for t in range(depth + len(pipelines) - 1):
        for j, stages in enumerate(pipelines):
            if 0 <= t - j < depth:
                stages[t - j]()


def _swiglu_stages(st, g_pre_ref, g_post_ref, wg_ref, wu_ref, wd_ref):
    def pre():
        st["u"] = _rmsnorm(st["h"], g_pre_ref[...]).astype(BF16)

    def gate_up():
        u = st.pop("u")
        st["gate"], st["up"] = _mm(u, wg_ref[...]), _mm(u, wu_ref[...])

    def activate():
        gate = st.pop("gate")
        st["act"] = (gate * jax.nn.sigmoid(gate) * st.pop("up")).astype(BF16)

    def down():
        st["y"] = _mm(st.pop("act"), wd_ref[...])

    def post():
        st["h"] = st["h"] + 0.5 * _rmsnorm(st.pop("y"), g_post_ref[...])

    return [pre, gate_up, activate, down, post]


def _merge_stages(first, second):
    last, head = first[-1], second[0]

    def both():
        last()
        head()

    return first[:-1] + [both] + second[1:]


def _rope(z, cos, sin_signed):
    rows, width = z.shape
    lane = lax.broadcasted_iota(jnp.int32, (rows, LANES), 1)
    first_half = (lane & HALF) == 0
    outs = []
    for c in range(width // LANES):
        zc = z[:, c * LANES:(c + 1) * LANES]
        partner = jnp.where(first_half,
                            pltpu.roll(zc, LANES - HALF, 1),
                            pltpu.roll(zc, HALF, 1))
        outs.append(zc * cos + partner * sin_signed)
    return outs[0] if len(outs) == 1 else jnp.concatenate(outs, axis=1)


_IN_WIDTHS = (QA_W, QA_W, QA_W, QB_W, KVB_W, KVB_W)


def _project_stages(st, rows, g_pre_ref, win_ref, cos_ref, sin_ref, emit):
    def pre():
        st["u"] = _rmsnorm(st["h"], g_pre_ref[...]).astype(BF16)

    def project():
        u, lo, pieces = st.pop("u"), 0, []
        for width in _IN_WIDTHS:
            pieces.append(_mm(u, win_ref[:, lo:lo + width]))
            lo += width
        st["z"] = pieces

    def rotate():
        qa, ka, va, qb, kb, vb = st.pop("z")
        cos, sin = cos_ref[rows, :], sin_ref[rows, :]
        emit(_rope(qa, cos, sin) * SCALE, _rope(ka, cos, sin), va,
             _rope(qb, cos, sin) * SCALE, _rope(kb, cos, sin), vb)

    return [pre, project, rotate]


def _stage1_stages(st, rows, x_ref, cos_ref, sin_ref, g1a_ref, g1b_ref, gmix_ref,
                   wg_ref, wu_ref, wd_ref, win_ref, h_ref, emit):
    def load():
        st["h"] = x_ref[rows, :]

    def store():
        h_ref[rows, :] = st["h"]

    ffn = _swiglu_stages(st, g1a_ref, g1b_ref, wg_ref, wu_ref, wd_ref)
    proj = _project_stages(st, rows, gmix_ref, win_ref, cos_ref, sin_ref, emit)
    return _merge_stages(_merge_stages([load], ffn), _merge_stages([store], proj))


def _dup_heads(x):
    lane = lax.broadcasted_iota(jnp.int32, x.shape, 1)
    swapped = pltpu.roll(x, HEAD_DIM, 1)
    lo = lane < HEAD_DIM
    return jnp.where(lo, x, swapped), jnp.where(lo, swapped, x)


def _stage1_prompt_kernel(x_ref, cos_ref, sin_ref, g1a_ref, g1b_ref, gmix_ref,
                          wg_ref, wu_ref, wd_ref, win_ref,
                          h_ref, qa1_ref, ka1_ref, va1_ref, qa4_ref, ka4_ref, va4_ref,
                          qa16_ref, ka16_ref, va16_ref, qb_ref, kb2_ref, vb2_ref,
                          nak_ref, nav_ref, nbk_ref, nbv_ref, stage_ref):
    tm = x_ref.shape[0]
    bm = min(ROW_BLOCK, tm)
    nblocks = tm // bm

    def emitter(j):
        rows = slice(j * bm, (j + 1) * bm)

        def decimated(z, nat_ref, dec_refs):
            nat_ref[rows, :] = z.astype(BF16)
            for c in range(QA_W // LANES):
                stage_ref[j, c] = z[:, c * LANES:(c + 1) * LANES]
            for d, ref in dec_refs:
                n = bm // d
                for r in range(d):
                    for c in range(QA_W // LANES):
                        lo = r * QA_W + c * LANES
                        piece = stage_ref[j, c, pl.ds(r, n, stride=d), :]
                        ref[j * n:(j + 1) * n, lo:lo + LANES] = piece.astype(BF16)

        def emit(qa, ka, va, qb, kb, vb):
            decimated(qa, qa1_ref, ((4, qa4_ref), (16, qa16_ref)))
            decimated(ka, ka1_ref, ((4, ka4_ref), (16, ka16_ref)))
            decimated(va, va1_ref, ((4, va4_ref), (16, va16_ref)))
            qb_ref[rows, :] = qb.astype(BF16)
            k0, k1 = _dup_heads(kb)
            kb2_ref[rows, 0:LANES] = k0.astype(BF16)
            kb2_ref[rows, LANES:2 * LANES] = k1.astype(BF16)
            v0, v1 = _dup_heads(vb)
            vb2_ref[rows, 0:LANES] = v0.astype(BF16)
            vb2_ref[rows, LANES:2 * LANES] = v1.astype(BF16)
            nak_ref[rows, :] = ka
            nav_ref[rows, :] = va
            if j == nblocks - 1:
                nbk_ref[...] = kb[bm - WIN_B:, :]
                nbv_ref[...] = vb[bm - WIN_B:, :]

        return rows, emit

    pipelines = []
    for j in range(nblocks):
        rows, emit = emitter(j)
        pipelines.append(_stage1_stages({}, rows, x_ref, cos_ref, sin_ref, g1a_ref, g1b_ref, gmix_ref,
                                        wg_ref, wu_ref, wd_ref, win_ref, h_ref, emit))
    _run_staggered(pipelines)


def _stage1_sample_kernel(x_ref, cos_ref, sin_ref, g1a_ref, g1b_ref, gmix_ref,
                          wg_ref, wu_ref, wd_ref, win_ref,
                          h_ref, qa_ref, ka_ref, va_ref, qb_ref, kb_ref, vb_ref):
    def emit(*pieces):
        for ref, z in zip((qa_ref, ka_ref, va_ref, qb_ref, kb_ref, vb_ref), pieces):
            ref[...] = z

    _run_staggered([_stage1_stages({}, slice(None), x_ref, cos_ref, sin_ref, g1a_ref, g1b_ref,
                                   gmix_ref, wg_ref, wu_ref, wd_ref, win_ref, h_ref, emit)])


def _stage1_weight_specs():
    return [_whole((1, D_MODEL))] * 3 + [
        _whole((D_MODEL, D_FF)), _whole((D_MODEL, D_FF)), _whole((D_FF, D_MODEL)),
        _whole((D_MODEL, IN_W))]


def _stage1_prompt(x, cos, sin, g1a, g1b, gmix, wg, wu, wd, win):
    nb, seq, _ = x.shape
    tm = ROW_TILE
    nt = seq // tm
    first_kept = (seq - WIN_A) // tm

    def rows(width, dtype, d=1):
        shape = jax.ShapeDtypeStruct((nb, seq // d, d * width), dtype)
        spec = pl.BlockSpec((None, tm // d, d * width), lambda b, i: (b, i, 0))
        return shape, spec

    outs = [rows(D_MODEL, F32)]
    outs += [rows(QA_W, BF16)] * 3 + [rows(QA_W, BF16, 4)] * 3 + [rows(QA_W, BF16, 16)] * 3
    outs += [rows(QB_W, BF16), rows(2 * LANES, BF16), rows(2 * LANES, BF16)]
    kept = (jax.ShapeDtypeStruct((nb, WIN_A, QA_W), F32),
            pl.BlockSpec((None, tm, QA_W), lambda b, i: (b, jnp.maximum(i - first_kept, 0), 0)))
    last = (jax.ShapeDtypeStruct((nb, WIN_B, KVB_W), F32),
            pl.BlockSpec((None, WIN_B, KVB_W), lambda b, i: (b, 0, 0)))
    outs += [kept, kept, last, last]
    in_specs = [pl.BlockSpec((None, tm, D_MODEL), lambda b, i: (b, i, 0)),
                pl.BlockSpec((tm, LANES), lambda b, i: (i, 0)),
                pl.BlockSpec((tm, LANES), lambda b, i: (i, 0))] + _stage1_weight_specs()
    return pl.pallas_call(
        _stage1_prompt_kernel,
        out_shape=[o[0] for o in outs],
        grid=(nb, nt),
        in_specs=in_specs,
        out_specs=[o[1] for o in outs],
        scratch_shapes=[pltpu.VMEM((tm // ROW_BLOCK, QA_W // LANES, ROW_BLOCK, LANES), F32)],
        compiler_params=pltpu.CompilerParams(
            dimension_semantics=("arbitrary", "arbitrary"), vmem_limit_bytes=VMEM_LIMIT),
        name="stage1_prompt",
    )(x, cos, sin, g1a, g1b, gmix, wg, wu, wd, win)


def _stage1_sample(x, cos, sin, g1a, g1b, gmix, wg, wu, wd, win):
    n = x.shape[0]
    widths = (D_MODEL, QA_W, QA_W, QA_W, QB_W, KVB_W, KVB_W)
    return pl.pallas_call(
        _stage1_sample_kernel,
        out_shape=[jax.ShapeDtypeStruct((n, w), F32) for w in widths],
        grid=(1,),
        in_specs=[_whole((n, D_MODEL)), _whole((n, LANES)), _whole((n, LANES))]
        + _stage1_weight_specs(),
        out_specs=[pl.BlockSpec((n, w), lambda i: (0, 0)) for w in widths],
        compiler_params=pltpu.CompilerParams(
            dimension_semantics=("arbitrary",), vmem_limit_bytes=VMEM_LIMIT),
        name="stage1_sample",
    )(x, cos, sin, g1a, g1b, gmix, wg, wu, wd, win)


def _banded_attention_kernel(*refs, with_sink, with_lse):
    refs = list(refs)
    sink_ref = refs.pop(0) if with_sink else None
    q_ref, kp_ref, kc_ref, vp_ref, vc_ref, o_ref = refs[:6]
    lse_ref = refs[6] if with_lse else None
    rows = q_ref.shape[0]
    col_block = pl.program_id(1)
    has_prev = pl.program_id(2) > 0

    nblk = rows // QBLK
    keys = 2 * QBLK

    q = q_ref[...]
    low_lanes = lax.broadcasted_iota(jnp.int32, (rows, LANES), 1) < HEAD_DIM
    zero = jnp.zeros_like(q)
    q2 = jnp.concatenate([jnp.where(low_lanes, q, zero).reshape(nblk, QBLK, LANES),
                          jnp.where(low_lanes, zero, q).reshape(nblk, QBLK, LANES)], axis=1)
    k_all = jnp.concatenate([kp_ref[...], kc_ref[...]], axis=0)
    v_all = jnp.concatenate([vp_ref[...], vc_ref[...]], axis=0)
    k2 = jnp.stack([k_all[b * QBLK:b * QBLK + keys] for b in range(nblk)])
    ones = jnp.ones((keys, LANES), BF16)
    v2 = jnp.stack([jnp.concatenate([v_all[b * QBLK:b * QBLK + keys], ones], axis=1)
                    for b in range(nblk)])

    s = jnp.einsum("bqd,bkd->bqk", q2, k2, preferred_element_type=F32)
    qrow = lax.broadcasted_iota(jnp.int32, (keys, keys), 0) & (QBLK - 1)
    col = lax.broadcasted_iota(jnp.int32, (keys, keys), 1)
    in_band = jnp.logical_and(col >= qrow, col <= qrow + QBLK)
    s = jnp.where(in_band[None], s, -jnp.inf)
    first_ok = jnp.logical_or(has_prev, col >= QBLK)
    s = jnp.concatenate([jnp.where(first_ok[None], s[0:1], -jnp.inf), s[1:]], axis=0)
    m = jnp.max(s, axis=2, keepdims=True)
    if with_sink:
        head1 = lax.broadcasted_iota(jnp.int32, (keys, 1), 0) >= QBLK
        sink = jnp.where(head1, sink_ref[2 * col_block + 1], sink_ref[2 * col_block])[None]
        m = jnp.maximum(m, sink)
    e = jnp.exp(s - m).astype(BF16)
    pv = jnp.einsum("bqk,bkd->bqd", e, v2, preferred_element_type=F32)
    acc, den = pv[:, :, :LANES], pv[:, :, LANES:]
    if with_sink:
        den = den + jnp.exp(sink - m)
    out2 = acc / den
    low3 = low_lanes.reshape(nblk, QBLK, LANES)
    out = jnp.where(low3, out2[:, :QBLK], out2[:, QBLK:])
    o_ref[...] = out.reshape(rows, LANES).astype(o_ref.dtype)
    if with_lse:
        lse2 = m + jnp.log(den)
        lse_ref[...] = jnp.where(low3, lse2[:, :QBLK], lse2[:, QBLK:]).reshape(rows, LANES)


def _banded_attention(q, k, v, *, k_col, sinks=None, with_lse, name):
    nb, length, width = q.shape
    rows = min(ATT_ROWS, length)
    sub = rows // QBLK
    grid = (nb, width // LANES, length // rows)
    q_spec = pl.BlockSpec((None, rows, LANES), lambda b, c, i: (b, i, c))
    cur_spec = pl.BlockSpec((None, rows, LANES), lambda b, c, i: (b, i, k_col(c)))
    prev_spec = pl.BlockSpec((None, QBLK, LANES),
                             lambda b, c, i: (b, jnp.maximum(i * sub - 1, 0), k_col(c)))
    in_specs = [q_spec, prev_spec, cur_spec, prev_spec, cur_spec]
    args = [q, k, k, v, v]
    if sinks is not None:
        in_specs = [pl.BlockSpec(memory_space=pltpu.SMEM)] + in_specs
        args = [sinks] + args
    out_shape = [jax.ShapeDtypeStruct(q.shape, BF16)]
    out_specs = [q_spec]
    if with_lse:
        out_shape.append(jax.ShapeDtypeStruct(q.shape, F32))
        out_specs.append(q_spec)
    return pl.pallas_call(
        functools.partial(_banded_attention_kernel, with_sink=sinks is not None, with_lse=with_lse),
        out_shape=out_shape,
        grid=grid,
        in_specs=in_specs,
        out_specs=out_specs,
        compiler_params=pltpu.CompilerParams(
            dimension_semantics=("arbitrary",) * 3, vmem_limit_bytes=VMEM_LIMIT),
        name=name,
    )(*args)


def _slide_window(buf_t, new_col):
    length = buf_t.shape[1]
    lane = lax.broadcasted_iota(jnp.int32, buf_t.shape, 1)
    return jnp.where(lane == length - 1, new_col, pltpu.roll(buf_t, length - 1, 1))


_COL_QA, _COL_KA, _COL_VA, _COL_QB, _COL_KB, _COL_VB = 0, 4, 8, 12, 16, 17
_COLS_IN = 18
_COLS_OUT = 8


def _sample_attention_kernel(sink_ref, cols_ref, ck_ref, cv_ref, cbk_ref, cbv_ref,
                             out_ref, nak_ref, nav_ref, nbk_ref, nbv_ref):
    kv_head = pl.program_id(1)
    heads, _, la = ck_ref.shape
    cols = cols_ref[...]

    def columns(lo, n):
        return jnp.stack([cols[:, lo + j:lo + j + 1] for j in range(n)])

    dist = la - lax.broadcasted_iota(jnp.int32, (1, 1, la), 2)
    count = jnp.zeros((1, 1, la), F32)
    for d in DILATIONS:
        in_pattern = jnp.logical_and((dist & (d - 1)) == 0, dist <= d * WIN_STEPS)
        count = count + in_pattern.astype(F32)
    q, k_new, v_new = columns(_COL_QA, heads), columns(_COL_KA, heads), columns(_COL_VA, heads)
    k_t, v_t = ck_ref[...], cv_ref[...]
    s = jnp.sum(q * k_t, axis=1, keepdims=True)
    s = jnp.where(count > 0.0, s, -jnp.inf)
    s_new = jnp.sum(q * k_new, axis=1, keepdims=True)
    m = jnp.maximum(jnp.max(s, axis=2, keepdims=True), s_new)
    e = jnp.exp(s - m) * count
    e_new = float(len(DILATIONS)) * jnp.exp(s_new - m)
    den = jnp.sum(e, axis=2, keepdims=True) + e_new
    out_a = (jnp.sum(v_t * e, axis=2, keepdims=True) + e_new * v_new) / den

    qb = columns(_COL_QB, GROUP_B)
    kb_new, vb_new = columns(_COL_KB, 1), columns(_COL_VB, 1)
    kb_t, vb_t = cbk_ref[...][None], cbv_ref[...][None]
    gidx = lax.broadcasted_iota(jnp.int32, (GROUP_B, 1, 1), 0)
    sink = jnp.zeros((GROUP_B, 1, 1), F32)
    for g in range(GROUP_B):
        sink = jnp.where(gidx == g, sink_ref[kv_head * GROUP_B + g], sink)
    sb = jnp.sum(qb * kb_t, axis=1, keepdims=True)
    sb_new = jnp.sum(qb * kb_new, axis=1, keepdims=True)
    mb = jnp.maximum(jnp.maximum(jnp.max(sb, axis=2, keepdims=True), sb_new), sink)
    eb = jnp.exp(sb - mb)
    eb_new = jnp.exp(sb_new - mb)
    den_b = jnp.sum(eb, axis=2, keepdims=True) + eb_new + jnp.exp(sink - mb)
    out_b = (jnp.sum(vb_t * eb, axis=2, keepdims=True) + eb_new * vb_new) / den_b

    lane = lax.broadcasted_iota(jnp.int32, out_ref.shape, 1)
    packed = jnp.zeros(out_ref.shape, F32)
    for j in range(heads):
        packed = jnp.where(lane == j, out_a[j], packed)
    for j in range(GROUP_B):
        packed = jnp.where(lane == heads + j, out_b[j], packed)
    out_ref[...] = packed

    for h in range(heads):
        nak_ref[h] = _slide_window(k_t[h], k_new[h])
        nav_ref[h] = _slide_window(v_t[h], v_new[h])
    nbk_ref[...] = _slide_window(kb_t[0], kb_new[0])
    nbv_ref[...] = _slide_window(vb_t[0], vb_new[0])


def _sample_attention(sinks, qa, ka, va, qb, kb, vb, cak_t, cav_t, cbk_t, cbv_t):
    n, _, _, la = cak_t.shape
    lb = cbk_t.shape[-1]
    heads = N_HEADS_A // N_KV_B
    assert heads == GROUP_B
    by_group = lambda z: z.reshape(n, N_KV_B, -1, HEAD_DIM)
    cols = jnp.concatenate([by_group(z) for z in (qa, ka, va, qb, kb, vb)], axis=2)
    cols = cols.transpose(0, 1, 3, 2)

    def group_spec(*tail):
        return pl.BlockSpec((None, None) + tail, lambda b, g: (b, g) + (0,) * len(tail))

    def heads_spec(minor):
        return pl.BlockSpec((None, heads, HEAD_DIM, minor), lambda b, g: (b, g, 0, 0))

    in_specs = [pl.BlockSpec(memory_space=pltpu.SMEM), group_spec(HEAD_DIM, _COLS_IN),
                heads_spec(la), heads_spec(la), group_spec(HEAD_DIM, lb), group_spec(HEAD_DIM, lb)]
    out_shape = [jax.ShapeDtypeStruct((n, N_KV_B, HEAD_DIM, _COLS_OUT), F32),
                 jax.ShapeDtypeStruct(cak_t.shape, F32), jax.ShapeDtypeStruct(cav_t.shape, F32),
                 jax.ShapeDtypeStruct(cbk_t.shape, F32), jax.ShapeDtypeStruct(cbv_t.shape, F32)]
    out_specs = [group_spec(HEAD_DIM, _COLS_OUT), heads_spec(la), heads_spec(la),
                 group_spec(HEAD_DIM, lb), group_spec(HEAD_DIM, lb)]
    packed, nak, nav, nbk, nbv = pl.pallas_call(
        _sample_attention_kernel,
        out_shape=out_shape,
        grid=(n, N_KV_B),
        in_specs=in_specs,
        out_specs=out_specs,
        compiler_params=pltpu.CompilerParams(
            dimension_semantics=("arbitrary", "arbitrary"), vmem_limit_bytes=VMEM_LIMIT),
        name="sample_attention",
    )(sinks, cols, cak_t, cav_t, cbk_t, cbv_t)
    rows = packed.transpose(0, 1, 3, 2)
    oa = rows[:, :, :heads].reshape(n, QA_W)
    ob = rows[:, :, heads:].reshape(n, QB_W)
    return oa, ob, nak, nav, nbk, nbv


def _stage3_stages(st, rows, mixed, h_ref, p_ref, norm_refs, weight_refs, out_ref):
    gmix_ref, g2a_ref, g2b_ref, gpa_ref, gpb_ref = norm_refs
    wout_ref, wg_ref, wu_ref, wd_ref, wpg_ref, wpp_ref = weight_refs

    def load():
        st["oa"], st["ob"] = mixed()

    def out_proj():
        st["y"] = (_mm(st.pop("oa"), wout_ref[0:QA_W, :])
                   + _mm(st.pop("ob"), wout_ref[QA_W:QA_W + QB_W, :]))

    def mix_residual():
        st["h"] = h_ref[rows, :] + _rmsnorm(st.pop("y"), gmix_ref[...])

    def ple_pre():
        st["u"] = _rmsnorm(st["h"], gpa_ref[...]).astype(BF16)
        st["p"] = p_ref[rows, :].astype(BF16)

    def ple_dots():
        st["gate"], st["proj"] = _mm(st.pop("u"), wpg_ref[...]), _mm(st.pop("p"), wpp_ref[...])

    def ple_residual():
        y = jax.nn.sigmoid(st.pop("gate")) * st.pop("proj")
        out_ref[rows, :] = st.pop("h") + _rmsnorm(y, gpb_ref[...])

    ffn = _swiglu_stages(st, g2a_ref, g2b_ref, wg_ref, wu_ref, wd_ref)
    head = _merge_stages([load, out_proj, mix_residual], ffn)
    return _merge_stages(head, [ple_pre, ple_dots, ple_residual])


def _stage3_prompt_kernel(h_ref, o1_ref, l1_ref, o4_ref, l4_ref, o16_ref, l16_ref, ob_ref, p_ref,
                          *rest):
    norm_refs, weight_refs, out_ref = rest[:5], rest[5:11], rest[11]
    o4s_ref, l4s_ref, o16s_ref, l16s_ref = rest[12:]
    tm = h_ref.shape[0]
    bm = min(ROW_BLOCK, tm)

    def mixer(j):
        rows = slice(j * bm, (j + 1) * bm)

        def mixed():
            for d, src, dst in ((4, o4_ref, o4s_ref), (4, l4_ref, l4s_ref),
                                (16, o16_ref, o16s_ref), (16, l16_ref, l16s_ref)):
                n = bm // d
                for r in range(d):
                    for c in range(QA_W // LANES):
                        lo = r * QA_W + c * LANES
                        piece = src[j * n:(j + 1) * n, lo:lo + LANES].astype(F32)
                        dst[j, c, pl.ds(r, n, stride=d), :] = piece
            pieces = []
            for c in range(QA_W // LANES):
                cols = slice(c * LANES, (c + 1) * LANES)
                l1, l4, l16 = l1_ref[rows, cols], l4s_ref[j, c], l16s_ref[j, c]
                m = jnp.maximum(l1, jnp.maximum(l4, l16))
                w1, w4, w16 = jnp.exp(l1 - m), jnp.exp(l4 - m), jnp.exp(l16 - m)
                mix = w1 * o1_ref[rows, cols].astype(F32) + w4 * o4s_ref[j, c] + w16 * o16s_ref[j, c]
                pieces.append(mix / (w1 + w4 + w16))
            return jnp.concatenate(pieces, axis=1).astype(BF16), ob_ref[rows, :]

        return rows, mixed

    pipelines = []
    for j in range(tm // bm):
        rows, mixed = mixer(j)
        pipelines.append(_stage3_stages({}, rows, mixed, h_ref, p_ref, norm_refs, weight_refs, out_ref))
    _run_staggered(pipelines)


def _stage3_sample_kernel(h_ref, oa_ref, ob_ref, p_ref, *rest):
    norm_refs, weight_refs, out_ref = rest[:5], rest[5:11], rest[11]
    mixed = lambda: (oa_ref[...].astype(BF16), ob_ref[...].astype(BF16))
    _run_staggered([_stage3_stages({}, slice(None), mixed, h_ref, p_ref, norm_refs, weight_refs,
                                   out_ref)])


def _stage3_weight_specs():
    return [_whole((1, D_MODEL))] * 5 + [
        _whole((QA_W + QB_W, D_MODEL)), _whole((D_MODEL, D_FF)), _whole((D_MODEL, D_FF)),
        _whole((D_FF, D_MODEL)), _whole((D_MODEL, D_MODEL)), _whole((D_PLE, D_MODEL))]


def _stage3_prompt(h, o1, l1, o4, l4, o16, l16, ob, p, norms, weights):
    nb, seq, _ = h.shape
    tm = ROW_TILE

    def rows(width, d=1):
        return pl.BlockSpec((None, tm // d, d * width), lambda b, i: (b, i, 0))

    in_specs = [rows(D_MODEL), rows(QA_W), rows(QA_W), rows(QA_W, 4), rows(QA_W, 4),
                rows(QA_W, 16), rows(QA_W, 16), rows(QB_W), rows(D_PLE)] + _stage3_weight_specs()
    return pl.pallas_call(
        _stage3_prompt_kernel,
        out_shape=jax.ShapeDtypeStruct(h.shape, F32),
        grid=(nb, seq // tm),
        in_specs=in_specs,
        out_specs=rows(D_MODEL),
        scratch_shapes=[pltpu.VMEM((tm // ROW_BLOCK, QA_W // LANES, ROW_BLOCK, LANES), F32)] * 4,
        compiler_params=pltpu.CompilerParams(
            dimension_semantics=("arbitrary", "arbitrary"), vmem_limit_bytes=VMEM_LIMIT),
        name="stage3_prompt",
    )(h, o1, l1, o4, l4, o16, l16, ob, p, *norms, *weights)


def _stage3_sample(h, oa, ob, p, norms, weights):
    n = h.shape[0]
    return pl.pallas_call(
        _stage3_sample_kernel,
        out_shape=jax.ShapeDtypeStruct(h.shape, F32),
        grid=(1,),
        in_specs=[_whole((n, D_MODEL)), _whole((n, QA_W)), _whole((n, QB_W)), _whole((n, D_PLE))]
        + _stage3_weight_specs(),
        out_specs=pl.BlockSpec((n, D_MODEL), lambda i: (0, 0)),
        compiler_params=pltpu.CompilerParams(
            dimension_semantics=("arbitrary",), vmem_limit_bytes=VMEM_LIMIT),
        name="stage3_sample",
    )(h, oa, ob, p, *norms, *weights)


def _rope_tables(pos):
    inv = jnp.power(ROPE_THETA, -jnp.arange(HALF, dtype=F32) * 2.0 / HEAD_DIM)
    ang = pos.astype(F32)[:, None] * inv[None, :]
    c, s = jnp.cos(ang), jnp.sin(ang)
    return jnp.concatenate([c, c, c, c], axis=-1), jnp.concatenate([-s, s, -s, s], axis=-1)


def _layer(i, hp, hs, caches, p_prompt, p_sample, norms, weights, sinks):
    (g1a, g1b, gmix_a, gmix_b, g2a, g2b, gpa, gpb) = [g[i][None, :] for g in norms]
    (wg1, wu1, wd1, win, wout, wg2, wu2, wd2, wpg, wpp) = [w[i].astype(BF16) for w in weights]
    cak, cav, cbk, cbv = [c[i] for c in caches]
    nb, seq, _ = hp.shape
    n_dec, dec_seq, _ = hs.shape

    cos_p, sin_p = _rope_tables(jnp.arange(seq, dtype=jnp.int32))
    pos_s = jnp.broadcast_to(PAST_LEN + jnp.arange(dec_seq, dtype=jnp.int32)[None, :],
                             (n_dec, dec_seq)).reshape(-1)
    cos_s, sin_s = _rope_tables(pos_s)
    stage1_w = (g1a, g1b, gmix_a, wg1, wu1, wd1, win)
    stage3_n = (gmix_b, g2a, g2b, gpa, gpb)
    stage3_w = (wout, wg2, wu2, wd2, wpg, wpp)

    (h1, qa1, ka1, va1, qa4, ka4, va4, qa16, ka16, va16, qb, kb2, vb2,
     nak_p, nav_p, nbk_p, nbv_p) = _stage1_prompt(hp, cos_p, sin_p, *stage1_w)
    same = lambda c: c
    o1, l1 = _banded_attention(qa1, ka1, va1, k_col=same, with_lse=True, name="dilated_d1")
    o4, l4 = _banded_attention(qa4, ka4, va4, k_col=same, with_lse=True, name="dilated_d4")
    o16, l16 = _banded_attention(qa16, ka16, va16, k_col=same, with_lse=True, name="dilated_d16")
    (ob,) = _banded_attention(qb, kb2, vb2, k_col=lambda c: c // 2, sinks=sinks[i].astype(F32),
                              with_lse=False, name="swa_sink")
    hp = _stage3_prompt(h1, o1, l1, o4, l4, o16, l16, ob, p_prompt[i], stage3_n, stage3_w)

    xs = hs.reshape(n_dec * dec_seq, D_MODEL)
    h1s, qa_s, ka_s, va_s, qb_s, kb_s, vb_s = _stage1_sample(xs, cos_s, sin_s, *stage1_w)
    rows_last = lambda z: z.transpose(0, 2, 3, 1)
    rows_first = lambda z: z.transpose(0, 3, 1, 2)
    oa_s, ob_s, nak_s, nav_s, nbk_s, nbv_s = _sample_attention(
        sinks[i].astype(F32), qa_s, ka_s, va_s, qb_s, kb_s, vb_s,
        rows_last(cak), rows_last(cav), rows_last(cbk), rows_last(cbv))
    hs = _stage3_sample(h1s, oa_s, ob_s,
                        p_sample[i].reshape(n_dec * dec_seq, D_PLE), stage3_n, stage3_w)
    hs = hs.reshape(n_dec, dec_seq, D_MODEL)

    heads5 = lambda z, h: z.reshape(z.shape[0], z.shape[1], h, HEAD_DIM)
    new = (heads5(nak_p, N_HEADS_A), heads5(nav_p, N_HEADS_A), heads5(nbk_p, N_KV_B),
           heads5(nbv_p, N_KV_B), rows_first(nak_s), rows_first(nav_s), rows_first(nbk_s),
           rows_first(nbv_s))
    return hp, hs, new


def kernel(x_prompt, x_sample, cache_a_k, cache_a_v, cache_b_k, cache_b_v, p_prompt, p_sample,
           norm_f1_pre, norm_f1_post, w_f1_gate, w_f1_up, w_f1_down,
           norm_mix_pre, norm_mix_post, w_in, sinks_b, w_out,
           norm_f2_pre, norm_f2_post, w_f2_gate, w_f2_up, w_f2_down,
           norm_ple_pre, norm_ple_post, w_ple_gate, w_ple_proj):
    assert x_sample.shape[1] == 1 and cache_a_k.shape[2] == WIN_A and cache_b_k.shape[2] == WIN_B
    norms = (norm_f1_pre, norm_f1_post, norm_mix_pre, norm_mix_post,
             norm_f2_pre, norm_f2_post, norm_ple_pre, norm_ple_post)
    weights = (w_f1_gate, w_f1_up, w_f1_down, w_in, w_out,
               w_f2_gate, w_f2_up, w_f2_down, w_ple_gate, w_ple_proj)
    caches = (cache_a_k, cache_a_v, cache_b_k, cache_b_v)
    hp, hs = x_prompt, x_sample
    per_layer = []
    for i in range(norm_f1_pre.shape[0]):
        hp, hs, new = _layer(i, hp, hs, caches, p_prompt, p_sample, norms, weights, sinks_b)
        per_layer.append(new)
    stacked = [jnp.stack([layer[j] for layer in per_layer]) for j in range(8)]
    return (hp, hs, *stacked)
```

```python
import functools

import jax
import jax.numpy as jnp
from jax import lax
from jax.experimental import pallas as pl
from jax.experimental.pallas import tpu as pltpu

F32 = jnp.float32
BF16 = jnp.bfloat16

D_MODEL = 1024
HEAD_DIM = 64
HALF = HEAD_DIM // 2
N_HEADS_A = 8
N_HEADS_B = 8
N_KV_B = 2
GROUP_B = N_HEADS_B // N_KV_B
DILATIONS = (1, 4, 16)
WIN_STEPS = 128
WIN_A = 2048
WIN_B = 128
PAST_LEN = 16384
D_FF = 2816
D_PLE = 256
ROPE_THETA = 10000.0
EPS = 1e-6
SCALE = HEAD_DIM ** -0.5
QA_W = N_HEADS_A * HEAD_DIM
QB_W = N_HEADS_B * HEAD_DIM
KVB_W = N_KV_B * HEAD_DIM
IN_W = 3 * QA_W + QB_W + 2 * KVB_W
LANES = 128
QBLK = 128

ROW_TILE = 128
ROW_BLOCK = 128
ATT_ROWS = 1024
VMEM_LIMIT = 56 * 1024 * 1024

K_PASS_COLS = 2 * N_HEADS_A
V_COL_QB, V_COL_KB, V_COL_VB = N_HEADS_A, N_HEADS_A + N_HEADS_B, N_HEADS_A + N_HEADS_B + N_KV_B
V_PASS_COLS = V_COL_VB + N_KV_B
OUT_COLS = N_HEADS_A + N_HEADS_B
STATS = LANES


def _whole(shape):
    nd = len(shape)
    return pl.BlockSpec(shape, lambda *_: (0,) * nd, pipeline_mode=pl.Buffered(1))


def _rmsnorm(x, g):
    return x * lax.rsqrt(jnp.mean(x * x, axis=-1, keepdims=True) + EPS) * g


def _mm(a, w):
    return jnp.dot(a, w, preferred_element_type=F32)


def _run_staggered(pipelines, extras=None):
    extras = extras or {}
    depth = len(pipelines[0])
    for t in range(depth + len(pipelines) - 1):
        for j, stages in enumerate(pipelines):
            if 0 <= t - j < depth:
                stages[t - j]()
        for piece in extras.get(t, ()):
            piece()


def _spread(pieces, shares):
    out, lo = {}, 0
    for step, count in shares:
        out[step] = pieces[lo:lo + count]
        lo += count
    assert lo == len(pieces)
    return out


def _swiglu_stages(st, g_pre_ref, g_post_ref, wg_ref, wu_ref, wd_ref):
    def pre():
        st["u"] = _rmsnorm(st["h"], g_pre_ref[...]).astype(BF16)

    def gate_up():
        u = st.pop("u")
        st["gate"], st["up"] = _mm(u, wg_ref[...]), _mm(u, wu_ref[...])

    def activate():
        gate = st.pop("gate")
        st["act"] = (gate * jax.nn.sigmoid(gate) * st.pop("up")).astype(BF16)

    def down():
        st["y"] = _mm(st.pop("act"), wd_ref[...])

    def post():
        st["h"] = st["h"] + 0.5 * _rmsnorm(st.pop("y"), g_post_ref[...])

    return [pre, gate_up, activate, down, post]


def _merge_stages(first, second):
    last, head = first[-1], second[0]

    def both():
        last()
        head()

    return first[:-1] + [both] + second[1:]


def _rope(z, cos, sin_signed):
    rows, width = z.shape
    lane = lax.broadcasted_iota(jnp.int32, (rows, LANES), 1)
    first_half = (lane & HALF) == 0
    outs = []
    for c in range(width // LANES):
        zc = z[:, c * LANES:(c + 1) * LANES]
        partner = jnp.where(first_half,
                            pltpu.roll(zc, LANES - HALF, 1),
                            pltpu.roll(zc, HALF, 1))
        outs.append(zc * cos + partner * sin_signed)
    return outs[0] if len(outs) == 1 else jnp.concatenate(outs, axis=1)


_IN_WIDTHS = (QA_W, QA_W, QA_W, QB_W, KVB_W, KVB_W)


def _project_stages(st, rows, g_pre_ref, win_ref, cos_ref, sin_ref, emit):
    def pre():
        st["u"] = _rmsnorm(st["h"], g_pre_ref[...]).astype(BF16)

    def project():
        u, lo, pieces = st.pop("u"), 0, []
        for width in _IN_WIDTHS:
            pieces.append(_mm(u, win_ref[:, lo:lo + width]))
            lo += width
        st["z"] = pieces

    def rotate():
        qa, ka, va, qb, kb, vb = st.pop("z")
        cos, sin = cos_ref[rows, :], sin_ref[rows, :]
        emit(_rope(qa, cos, sin) * SCALE, _rope(ka, cos, sin), va,
             _rope(qb, cos, sin) * SCALE, _rope(kb, cos, sin), vb)

    return [pre, project, rotate]


def _stage1_stages(st, rows, x_ref, cos_ref, sin_ref, g1a_ref, g1b_ref, gmix_ref,
                   wg_ref, wu_ref, wd_ref, win_ref, h_ref, emit):
    def load():
        st["h"] = x_ref[rows, :]

    def store():
        h_ref[rows, :] = st["h"]

    ffn = _swiglu_stages(st, g1a_ref, g1b_ref, wg_ref, wu_ref, wd_ref)
    proj = _project_stages(st, rows, gmix_ref, win_ref, cos_ref, sin_ref, emit)
    return _merge_stages(_merge_stages([load], ffn), _merge_stages([store], proj))


def _dup_heads(x):
    lane = lax.broadcasted_iota(jnp.int32, x.shape, 1)
    swapped = pltpu.roll(x, HEAD_DIM, 1)
    lo = lane < HEAD_DIM
    return jnp.where(lo, x, swapped), jnp.where(lo, swapped, x)


def _slide_window(buf_t, new_col):
    length = buf_t.shape[1]
    lane = lax.broadcasted_iota(jnp.int32, buf_t.shape, 1)
    return jnp.where(lane == length - 1, new_col, pltpu.roll(buf_t, length - 1, 1))


def _key_pass_pieces(cols_ref, ck_ref, w_ref, nak_ref):
    heads, _, la = ck_ref.shape
    shared = {}

    def pattern_count():
        if "count" not in shared:
            dist = la - lax.broadcasted_iota(jnp.int32, (1, la), 1)
            count = jnp.zeros((1, la), F32)
            for d in DILATIONS:
                in_pattern = jnp.logical_and((dist & (d - 1)) == 0, dist <= d * WIN_STEPS)
                count = count + in_pattern.astype(F32)
            shared["count"] = count
        return shared["count"]

    def piece(h):
        def run():
            count = pattern_count()
            q = cols_ref[:, h:h + 1]
            k_new = cols_ref[:, heads + h:heads + h + 1]
            k_t = ck_ref[h]
            s = jnp.sum(q * k_t, axis=0, keepdims=True)
            s = jnp.where(count > 0.0, s, -jnp.inf)
            s_new = jnp.sum(q * k_new, axis=0, keepdims=True)
            m = jnp.maximum(jnp.max(s, axis=1, keepdims=True), s_new)
            e = jnp.exp(s - m) * count
            e_new = float(len(DILATIONS)) * jnp.exp(s_new - m)
            den = jnp.sum(e, axis=1, keepdims=True) + e_new
            w_ref[h:h + 1, 0:la] = e
            lane = lax.broadcasted_iota(jnp.int32, (1, STATS), 1)
            w_ref[h:h + 1, la:la + STATS] = jnp.where(lane < STATS // 2, e_new, den)
            nak_ref[h] = _slide_window(k_t, k_new)

        return run

    return [piece(h) for h in range(heads)]


def _value_pass_pieces(sink_ref, cols_ref, w_ref, cv_ref, cbk_ref, cbv_ref,
                       out_ref, nav_ref, nbk_ref, nbv_ref):
    heads, _, la = cv_ref.shape

    def piece(h):
        def run():
            e = w_ref[h:h + 1, 0:la]
            e_new = w_ref[h:h + 1, la:la + 1]
            den = w_ref[h:h + 1, la + STATS // 2:la + STATS // 2 + 1]
            v_new = cols_ref[:, h:h + 1]
            v_t = cv_ref[h]
            acc = jnp.sum(v_t * e, axis=1, keepdims=True) + e_new * v_new
            out_ref[:, h:h + 1] = acc / den
            nav_ref[h] = _slide_window(v_t, v_new)

        return run

    def group_b():
        gidx = lax.broadcasted_iota(jnp.int32, (GROUP_B, 1, 1), 0)
        for kv in range(N_KV_B):
            k_t, v_t = cbk_ref[kv], cbv_ref[kv]
            k_new = cols_ref[:, V_COL_KB + kv:V_COL_KB + kv + 1]
            v_new = cols_ref[:, V_COL_VB + kv:V_COL_VB + kv + 1]
            first = V_COL_QB + kv * GROUP_B
            q = jnp.stack([cols_ref[:, first + g:first + g + 1] for g in range(GROUP_B)])
            sink = jnp.zeros((GROUP_B, 1, 1), F32)
            for g in range(GROUP_B):
                sink = jnp.where(gidx == g, sink_ref[kv * GROUP_B + g], sink)
            s = jnp.sum(q * k_t[None], axis=1, keepdims=True)
            s_new = jnp.sum(q * k_new[None], axis=1, keepdims=True)
            m = jnp.maximum(jnp.maximum(jnp.max(s, axis=2, keepdims=True), s_new), sink)
            e = jnp.exp(s - m)
            e_new = jnp.exp(s_new - m)
            den = jnp.sum(e, axis=2, keepdims=True) + e_new + jnp.exp(sink - m)
            out = (jnp.sum(v_t[None] * e, axis=2, keepdims=True) + e_new * v_new[None]) / den
            for g in range(GROUP_B):
                lane = N_HEADS_A + kv * GROUP_B + g
                out_ref[:, lane:lane + 1] = out[g]
            nbk_ref[kv] = _slide_window(k_t, k_new)
            nbv_ref[kv] = _slide_window(v_t, v_new)

    return [piece(h) for h in range(heads)] + [group_b]


def _columns(pieces):
    n = pieces[0].shape[0]
    return jnp.concatenate(pieces, axis=1).reshape(n, -1, HEAD_DIM).transpose(0, 2, 1)


def _stage1_prompt_kernel(x_ref, cos_ref, sin_ref, g1a_ref, g1b_ref, gmix_ref,
                          wg_ref, wu_ref, wd_ref, win_ref, cols_ref, ck_ref,
                          h_ref, qa1_ref, ka1_ref, va1_ref, qa4_ref, ka4_ref, va4_ref,
                          qa16_ref, ka16_ref, va16_ref, qb_ref, kb2_ref, vb2_ref,
                          nak_ref, nav_ref, nbk_ref, nbv_ref, w_ref, sak_ref, stage_ref):
    tm = x_ref.shape[0]
    bm = min(ROW_BLOCK, tm)
    nblocks = tm // bm

    def emitter(j):
        rows = slice(j * bm, (j + 1) * bm)

        def decimated(z, nat_ref, dec_refs):
            nat_ref[rows, :] = z.astype(BF16)
            for c in range(QA_W // LANES):
                stage_ref[j, c] = z[:, c * LANES:(c + 1) * LANES]
            for d, ref in dec_refs:
                n = bm // d
                for r in range(d):
                    for c in range(QA_W // LANES):
                        lo = r * QA_W + c * LANES
                        piece = stage_ref[j, c, pl.ds(r, n, stride=d), :]
                        ref[j * n:(j + 1) * n, lo:lo + LANES] = piece.astype(BF16)

        def emit(qa, ka, va, qb, kb, vb):
            decimated(qa, qa1_ref, ((4, qa4_ref), (16, qa16_ref)))
            decimated(ka, ka1_ref, ((4, ka4_ref), (16, ka16_ref)))
            decimated(va, va1_ref, ((4, va4_ref), (16, va16_ref)))
            qb_ref[rows, :] = qb.astype(BF16)
            k0, k1 = _dup_heads(kb)
            kb2_ref[rows, 0:LANES] = k0.astype(BF16)
            kb2_ref[rows, LANES:2 * LANES] = k1.astype(BF16)
            v0, v1 = _dup_heads(vb)
            vb2_ref[rows, 0:LANES] = v0.astype(BF16)
            vb2_ref[rows, LANES:2 * LANES] = v1.astype(BF16)
            nak_ref[rows, :] = ka
            nav_ref[rows, :] = va
            if j == nblocks - 1:
                nbk_ref[...] = kb[bm - WIN_B:, :]
                nbv_ref[...] = vb[bm - WIN_B:, :]

        return rows, emit

    pipelines = []
    for j in range(nblocks):
        rows, emit = emitter(j)
        pipelines.append(_stage1_stages({}, rows, x_ref, cos_ref, sin_ref, g1a_ref, g1b_ref, gmix_ref,
                                        wg_ref, wu_ref, wd_ref, win_ref, h_ref, emit))
    pieces = _key_pass_pieces(cols_ref, ck_ref, w_ref, sak_ref)
    _run_staggered(pipelines, _spread(pieces, ((1, 4), (3, 2), (5, 2))))


def _stage1_sample_kernel(x_ref, cos_ref, sin_ref, g1a_ref, g1b_ref, gmix_ref,
                          wg_ref, wu_ref, wd_ref, win_ref,
                          h_ref, qa_ref, ka_ref, va_ref, qb_ref, kb_ref, vb_ref):
    def emit(*pieces):
        for ref, z in zip((qa_ref, ka_ref, va_ref, qb_ref, kb_ref, vb_ref), pieces):
            ref[...] = z

    _run_staggered([_stage1_stages({}, slice(None), x_ref, cos_ref, sin_ref, g1a_ref, g1b_ref,
                                   gmix_ref, wg_ref, wu_ref, wd_ref, win_ref, h_ref, emit)])


def _stage1_weight_specs():
    return [_whole((1, D_MODEL))] * 3 + [
        _whole((D_MODEL, D_FF)), _whole((D_MODEL, D_FF)), _whole((D_FF, D_MODEL)),
        _whole((D_MODEL, IN_W))]


def _per_sequence_spec(tiles_per_batch, *block):
    return pl.BlockSpec((None,) + block,
                        lambda b, i: (b * tiles_per_batch + i,) + (0,) * len(block))


def _stage1_prompt(x, cos, sin, g1a, g1b, gmix, wg, wu, wd, win, cols_k, cak_t):
    nb, seq, _ = x.shape
    tm = ROW_TILE
    nt = seq // tm
    n, heads, _, la = cak_t.shape
    assert nb * nt == n and tm >= WIN_B
    first_kept = (seq - WIN_A) // tm

    def rows(width, dtype, d=1):
        shape = jax.ShapeDtypeStruct((nb, seq // d, d * width), dtype)
        spec = pl.BlockSpec((None, tm // d, d * width), lambda b, i: (b, i, 0))
        return shape, spec

    outs = [rows(D_MODEL, F32)]
    outs += [rows(QA_W, BF16)] * 3 + [rows(QA_W, BF16, 4)] * 3 + [rows(QA_W, BF16, 16)] * 3
    outs += [rows(QB_W, BF16), rows(2 * LANES, BF16), rows(2 * LANES, BF16)]
    kept = (jax.ShapeDtypeStruct((nb, WIN_A, QA_W), F32),
            pl.BlockSpec((None, tm, QA_W), lambda b, i: (b, jnp.maximum(i - first_kept, 0), 0)))
    last = (jax.ShapeDtypeStruct((nb, WIN_B, KVB_W), F32),
            pl.BlockSpec((None, WIN_B, KVB_W), lambda b, i: (b, 0, 0)))
    outs += [kept, kept, last, last]
    outs += [(jax.ShapeDtypeStruct((n, heads, la + STATS), F32), _per_sequence_spec(nt, heads, la + STATS)),
             (jax.ShapeDtypeStruct(cak_t.shape, F32), _per_sequence_spec(nt, heads, HEAD_DIM, la))]
    in_specs = ([pl.BlockSpec((None, tm, D_MODEL), lambda b, i: (b, i, 0)),
                 pl.BlockSpec((tm, LANES), lambda b, i: (i, 0)),
                 pl.BlockSpec((tm, LANES), lambda b, i: (i, 0))] + _stage1_weight_specs()
                + [_per_sequence_spec(nt, HEAD_DIM, K_PASS_COLS),
                   _per_sequence_spec(nt, heads, HEAD_DIM, la)])
    return pl.pallas_call(
        _stage1_prompt_kernel,
        out_shape=[o[0] for o in outs],
        grid=(nb, nt),
        in_specs=in_specs,
        out_specs=[o[1] for o in outs],
        scratch_shapes=[pltpu.VMEM((tm // ROW_BLOCK, QA_W // LANES, ROW_BLOCK, LANES), F32)],
        compiler_params=pltpu.CompilerParams(
            dimension_semantics=("arbitrary", "arbitrary"), vmem_limit_bytes=VMEM_LIMIT),
        name="stage1_prompt",
    )(x, cos, sin, g1a, g1b, gmix, wg, wu, wd, win, cols_k, cak_t)


def _stage1_sample(x, cos, sin, g1a, g1b, gmix, wg, wu, wd, win):
    n = x.shape[0]
    widths = (D_MODEL, QA_W, QA_W, QA_W, QB_W, KVB_W, KVB_W)
    return pl.pallas_call(
        _stage1_sample_kernel,
        out_shape=[jax.ShapeDtypeStruct((n, w), F32) for w in widths],
        grid=(1,),
        in_specs=[_whole((n, D_MODEL)), _whole((n, LANES)), _whole((n, LANES))]
        + _stage1_weight_specs(),
        out_specs=[pl.BlockSpec((n, w), lambda i: (0, 0)) for w in widths],
        compiler_params=pltpu.CompilerParams(
            dimension_semantics=("arbitrary",), vmem_limit_bytes=VMEM_LIMIT),
        name="stage1_sample",
    )(x, cos, sin, g1a, g1b, gmix, wg, wu, wd, win)


def _banded_attention_kernel(*refs, with_sink, with_lse):
    refs = list(refs)
    sink_ref = refs.pop(0) if with_sink else None
    q_ref, kp_ref, kc_ref, vp_ref, vc_ref, o_ref = refs[:6]
    lse_ref = refs[6] if with_lse else None
    rows = q_ref.shape[0]
    col_block = pl.program_id(1)
    has_prev = pl.program_id(2) > 0

    nblk = rows // QBLK
    keys = 2 * QBLK

    q = q_ref[...]
    low_lanes = lax.broadcasted_iota(jnp.int32, (rows, LANES), 1) < HEAD_DIM
    zero = jnp.zeros_like(q)
    q2 = jnp.concatenate([jnp.where(low_lanes, q, zero).reshape(nblk, QBLK, LANES),
                          jnp.where(low_lanes, zero, q).reshape(nblk, QBLK, LANES)], axis=1)
    k_all = jnp.concatenate([kp_ref[...], kc_ref[...]], axis=0)
    v_all = jnp.concatenate([vp_ref[...], vc_ref[...]], axis=0)
    k2 = jnp.stack([k_all[b * QBLK:b * QBLK + keys] for b in range(nblk)])
    ones = jnp.ones((keys, LANES), BF16)
    v2 = jnp.stack([jnp.concatenate([v_all[b * QBLK:b * QBLK + keys], ones], axis=1)
                    for b in range(nblk)])

    s = jnp.einsum("bqd,bkd->bqk", q2, k2, preferred_element_type=F32)
    qrow = lax.broadcasted_iota(jnp.int32, (keys, keys), 0) & (QBLK - 1)
    col = lax.broadcasted_iota(jnp.int32, (keys, keys), 1)
    in_band = jnp.logical_and(col >= qrow, col <= qrow + QBLK)
    s = jnp.where(in_band[None], s, -jnp.inf)
    first_ok = jnp.logical_or(has_prev, col >= QBLK)
    s = jnp.concatenate([jnp.where(first_ok[None], s[0:1], -jnp.inf), s[1:]], axis=0)
    m = jnp.max(s, axis=2, keepdims=True)
    if with_sink:
        head1 = lax.broadcasted_iota(jnp.int32, (keys, 1), 0) >= QBLK
        sink = jnp.where(head1, sink_ref[2 * col_block + 1], sink_ref[2 * col_block])[None]
        m = jnp.maximum(m, sink)
    e = jnp.exp(s - m).astype(BF16)
    pv = jnp.einsum("bqk,bkd->bqd", e, v2, preferred_element_type=F32)
    acc, den = pv[:, :, :LANES], pv[:, :, LANES:]
    if with_sink:
        den = den + jnp.exp(sink - m)
    out2 = acc / den
    low3 = low_lanes.reshape(nblk, QBLK, LANES)
    out = jnp.where(low3, out2[:, :QBLK], out2[:, QBLK:])
    o_ref[...] = out.reshape(rows, LANES).astype(o_ref.dtype)
    if with_lse:
        lse2 = m + jnp.log(den)
        lse_ref[...] = jnp.where(low3, lse2[:, :QBLK], lse2[:, QBLK:]).reshape(rows, LANES)


def _banded_attention(q, k, v, *, k_col, sinks=None, with_lse, name):
    nb, length, width = q.shape
    rows = min(ATT_ROWS, length)
    sub = rows // QBLK
    grid = (nb, width // LANES, length // rows)
    q_spec = pl.BlockSpec((None, rows, LANES), lambda b, c, i: (b, i, c))
    cur_spec = pl.BlockSpec((None, rows, LANES), lambda b, c, i: (b, i, k_col(c)))
    prev_spec = pl.BlockSpec((None, QBLK, LANES),
                             lambda b, c, i: (b, jnp.maximum(i * sub - 1, 0), k_col(c)))
    in_specs = [q_spec, prev_spec, cur_spec, prev_spec, cur_spec]
    args = [q, k, k, v, v]
    if sinks is not None:
        in_specs = [pl.BlockSpec(memory_space=pltpu.SMEM)] + in_specs
        args = [sinks] + args
    out_shape = [jax.ShapeDtypeStruct(q.shape, BF16)]
    out_specs = [q_spec]
    if with_lse:
        out_shape.append(jax.ShapeDtypeStruct(q.shape, F32))
        out_specs.append(q_spec)
    return pl.pallas_call(
        functools.partial(_banded_attention_kernel, with_sink=sinks is not None, with_lse=with_lse),
        out_shape=out_shape,
        grid=grid,
        in_specs=in_specs,
        out_specs=out_specs,
        compiler_params=pltpu.CompilerParams(
            dimension_semantics=("arbitrary",) * 3, vmem_limit_bytes=VMEM_LIMIT),
        name=name,
    )(*args)


def _stage3_stages(st, rows, mixed, h_ref, p_ref, norm_refs, weight_refs, out_ref):
    gmix_ref, g2a_ref, g2b_ref, gpa_ref, gpb_ref = norm_refs
    wout_ref, wg_ref, wu_ref, wd_ref, wpg_ref, wpp_ref = weight_refs

    def load():
        st["oa"], st["ob"] = mixed()

    def out_proj():
        st["y"] = (_mm(st.pop("oa"), wout_ref[0:QA_W, :])
                   + _mm(st.pop("ob"), wout_ref[QA_W:QA_W + QB_W, :]))

    def mix_residual():
        st["h"] = h_ref[rows, :] + _rmsnorm(st.pop("y"), gmix_ref[...])

    def ple_pre():
        st["u"] = _rmsnorm(st["h"], gpa_ref[...]).astype(BF16)
        st["p"] = p_ref[rows, :].astype(BF16)

    def ple_dots():
        st["gate"], st["proj"] = _mm(st.pop("u"), wpg_ref[...]), _mm(st.pop("p"), wpp_ref[...])

    def ple_residual():
        y = jax.nn.sigmoid(st.pop("gate")) * st.pop("proj")
        out_ref[rows, :] = st.pop("h") + _rmsnorm(y, gpb_ref[...])

    ffn = _swiglu_stages(st, g2a_ref, g2b_ref, wg_ref, wu_ref, wd_ref)
    head = _merge_stages([load, out_proj, mix_residual], ffn)
    return _merge_stages(head, [ple_pre, ple_dots, ple_residual])


def _stage3_prompt_kernel(sink_ref, h_ref, o1_ref, l1_ref, o4_ref, l4_ref, o16_ref, l16_ref,
                          ob_ref, p_ref, *rest):
    norm_refs, weight_refs = rest[:5], rest[5:11]
    cols_ref, w_ref, cv_ref, cbk_ref, cbv_ref = rest[11:16]
    out_ref, so_ref, sav_ref, sbk_ref, sbv_ref = rest[16:21]
    o4s_ref, l4s_ref, o16s_ref, l16s_ref = rest[21:]
    tm = h_ref.shape[0]
    bm = min(ROW_BLOCK, tm)

    def mixer(j):
        rows = slice(j * bm, (j + 1) * bm)

        def mixed():
            for d, src, dst in ((4, o4_ref, o4s_ref), (4, l4_ref, l4s_ref),
                                (16, o16_ref, o16s_ref), (16, l16_ref, l16s_ref)):
                n = bm // d
                for r in range(d):
                    for c in range(QA_W // LANES):
                        lo = r * QA_W + c * LANES
                        piece = src[j * n:(j + 1) * n, lo:lo + LANES].astype(F32)
                        dst[j, c, pl.ds(r, n, stride=d), :] = piece
            pieces = []
            for c in range(QA_W // LANES):
                cols = slice(c * LANES, (c + 1) * LANES)
                l1, l4, l16 = l1_ref[rows, cols], l4s_ref[j, c], l16s_ref[j, c]
                m = jnp.maximum(l1, jnp.maximum(l4, l16))
                w1, w4, w16 = jnp.exp(l1 - m), jnp.exp(l4 - m), jnp.exp(l16 - m)
                mix = w1 * o1_ref[rows, cols].astype(F32) + w4 * o4s_ref[j, c] + w16 * o16s_ref[j, c]
                pieces.append(mix / (w1 + w4 + w16))
            return jnp.concatenate(pieces, axis=1).astype(BF16), ob_ref[rows, :]

        return rows, mixed

    pipelines = []
    for j in range(tm // bm):
        rows, mixed = mixer(j)
        pipelines.append(_stage3_stages({}, rows, mixed, h_ref, p_ref, norm_refs, weight_refs, out_ref))
    pieces = _value_pass_pieces(sink_ref, cols_ref, w_ref, cv_ref, cbk_ref, cbv_ref,
                                so_ref, sav_ref, sbk_ref, sbv_ref)
    _run_staggered(pipelines, _spread(pieces, ((1, 1), (3, 4), (5, 2), (7, 2))))


def _stage3_sample_kernel(h_ref, o_ref, p_ref, *rest):
    norm_refs, weight_refs, out_ref = rest[:5], rest[5:11], rest[11]
    mixed = lambda: (o_ref[:, 0:QA_W].astype(BF16), o_ref[:, QA_W:QA_W + QB_W].astype(BF16))
    _run_staggered([_stage3_stages({}, slice(None), mixed, h_ref, p_ref, norm_refs, weight_refs,
                                   out_ref)])


def _stage3_weight_specs():
    return [_whole((1, D_MODEL))] * 5 + [
        _whole((QA_W + QB_W, D_MODEL)), _whole((D_MODEL, D_FF)), _whole((D_MODEL, D_FF)),
        _whole((D_FF, D_MODEL)), _whole((D_MODEL, D_MODEL)), _whole((D_PLE, D_MODEL))]


def _stage3_prompt(sinks, h, o1, l1, o4, l4, o16, l16, ob, p, norms, weights,
                   cols_v, w, cav_t, cbk_t, cbv_t):
    nb, seq, _ = h.shape
    tm = ROW_TILE
    nt = seq // tm
    n, heads, _, la = cav_t.shape
    lb = cbk_t.shape[-1]
    assert nb * nt == n

    def rows(width, d=1):
        return pl.BlockSpec((None, tm // d, d * width), lambda b, i: (b, i, 0))

    in_specs = ([pl.BlockSpec(memory_space=pltpu.SMEM),
                 rows(D_MODEL), rows(QA_W), rows(QA_W), rows(QA_W, 4), rows(QA_W, 4),
                 rows(QA_W, 16), rows(QA_W, 16), rows(QB_W), rows(D_PLE)] + _stage3_weight_specs()
                + [_per_sequence_spec(nt, HEAD_DIM, V_PASS_COLS),
                   _per_sequence_spec(nt, heads, la + STATS),
                   _per_sequence_spec(nt, heads, HEAD_DIM, la),
                   _per_sequence_spec(nt, N_KV_B, HEAD_DIM, lb),
                   _per_sequence_spec(nt, N_KV_B, HEAD_DIM, lb)])
    out_shape = [jax.ShapeDtypeStruct(h.shape, F32),
                 jax.ShapeDtypeStruct((n, HEAD_DIM, OUT_COLS), F32),
                 jax.ShapeDtypeStruct(cav_t.shape, F32),
                 jax.ShapeDtypeStruct(cbk_t.shape, F32), jax.ShapeDtypeStruct(cbv_t.shape, F32)]
    out_specs = [rows(D_MODEL), _per_sequence_spec(nt, HEAD_DIM, OUT_COLS),
                 _per_sequence_spec(nt, heads, HEAD_DIM, la),
                 _per_sequence_spec(nt, N_KV_B, HEAD_DIM, lb),
                 _per_sequence_spec(nt, N_KV_B, HEAD_DIM, lb)]
    return pl.pallas_call(
        _stage3_prompt_kernel,
        out_shape=out_shape,
        grid=(nb, nt),
        in_specs=in_specs,
        out_specs=out_specs,
        scratch_shapes=[pltpu.VMEM((tm // ROW_BLOCK, QA_W // LANES, ROW_BLOCK, LANES), F32)] * 4,
        compiler_params=pltpu.CompilerParams(
            dimension_semantics=("arbitrary", "arbitrary"), vmem_limit_bytes=VMEM_LIMIT),
        name="stage3_prompt",
    )(sinks, h, o1, l1, o4, l4, o16, l16, ob, p, *norms, *weights, cols_v, w, cav_t, cbk_t, cbv_t)


def _stage3_sample(h, o, p, norms, weights):
    n = h.shape[0]
    return pl.pallas_call(
        _stage3_sample_kernel,
        out_shape=jax.ShapeDtypeStruct(h.shape, F32),
        grid=(1,),
        in_specs=[_whole((n, D_MODEL)), _whole((n, QA_W + QB_W)), _whole((n, D_PLE))]
        + _stage3_weight_specs(),
        out_specs=pl.BlockSpec((n, D_MODEL), lambda i: (0, 0)),
        compiler_params=pltpu.CompilerParams(
            dimension_semantics=("arbitrary",), vmem_limit_bytes=VMEM_LIMIT),
        name="stage3_sample",
    )(h, o, p, *norms, *weights)


def _rope_tables(pos):
    inv = jnp.power(ROPE_THETA, -jnp.arange(HALF, dtype=F32) * 2.0 / HEAD_DIM)
    ang = pos.astype(F32)[:, None] * inv[None, :]
    c, s = jnp.cos(ang), jnp.sin(ang)
    return jnp.concatenate([c, c, c, c], axis=-1), jnp.concatenate([-s, s, -s, s], axis=-1)


def _layer(i, hp, hs, caches, p_prompt, p_sample, norms, weights, sinks):
    (g1a, g1b, gmix_a, gmix_b, g2a, g2b, gpa, gpb) = [g[i][None, :] for g in norms]
    (wg1, wu1, wd1, win, wout, wg2, wu2, wd2, wpg, wpp) = [w[i].astype(BF16) for w in weights]
    rows_last = lambda z: z.transpose(0, 2, 3, 1)
    rows_first = lambda z: z.transpose(0, 3, 1, 2)
    cak_t, cav_t, cbk_t, cbv_t = [rows_last(c[i]) for c in caches]
    nb, seq, _ = hp.shape
    n_dec, dec_seq, _ = hs.shape
    sink = sinks[i].astype(F32)

    cos_p, sin_p = _rope_tables(jnp.arange(seq, dtype=jnp.int32))
    pos_s = jnp.broadcast_to(PAST_LEN + jnp.arange(dec_seq, dtype=jnp.int32)[None, :],
                             (n_dec, dec_seq)).reshape(-1)
    cos_s, sin_s = _rope_tables(pos_s)
    stage1_w = (g1a, g1b, gmix_a, wg1, wu1, wd1, win)
    stage3_n = (gmix_b, g2a, g2b, gpa, gpb)
    stage3_w = (wout, wg2, wu2, wd2, wpg, wpp)

    xs = hs.reshape(n_dec * dec_seq, D_MODEL)
    h1s, qa_s, ka_s, va_s, qb_s, kb_s, vb_s = _stage1_sample(xs, cos_s, sin_s, *stage1_w)
    cols_k = _columns([qa_s, ka_s])
    cols_v = _columns([va_s, qb_s, kb_s, vb_s])

    (h1, qa1, ka1, va1, qa4, ka4, va4, qa16, ka16, va16, qb, kb2, vb2,
     nak_p, nav_p, nbk_p, nbv_p, w_s, nak_s) = _stage1_prompt(hp, cos_p, sin_p, *stage1_w,
                                                             cols_k, cak_t)
    same = lambda c: c
    o1, l1 = _banded_attention(qa1, ka1, va1, k_col=same, with_lse=True, name="dilated_d1")
    o4, l4 = _banded_attention(qa4, ka4, va4, k_col=same, with_lse=True, name="dilated_d4")
    o16, l16 = _banded_attention(qa16, ka16, va16, k_col=same, with_lse=True, name="dilated_d16")
    (ob,) = _banded_attention(qb, kb2, vb2, k_col=lambda c: c // 2, sinks=sink,
                              with_lse=False, name="swa_sink")
    hp, o_cols, nav_s, nbk_s, nbv_s = _stage3_prompt(
        sink, h1, o1, l1, o4, l4, o16, l16, ob, p_prompt[i], stage3_n, stage3_w,
        cols_v, w_s, cav_t, cbk_t, cbv_t)

    o_s = o_cols.transpose(0, 2, 1).reshape(n_dec, QA_W + QB_W)
    hs = _stage3_sample(h1s, o_s, p_sample[i].reshape(n_dec * dec_seq, D_PLE), stage3_n, stage3_w)
    hs = hs.reshape(n_dec, dec_seq, D_MODEL)

    heads5 = lambda z, h: z.reshape(z.shape[0], z.shape[1], h, HEAD_DIM)
    new = (heads5(nak_p, N_HEADS_A), heads5(nav_p, N_HEADS_A), heads5(nbk_p, N_KV_B),
           heads5(nbv_p, N_KV_B), rows_first(nak_s), rows_first(nav_s), rows_first(nbk_s),
           rows_first(nbv_s))
    return hp, hs, new


def kernel(x_prompt, x_sample, cache_a_k, cache_a_v, cache_b_k, cache_b_v, p_prompt, p_sample,
           norm_f1_pre, norm_f1_post, w_f1_gate, w_f1_up, w_f1_down,
           norm_mix_pre, norm_mix_post, w_in, sinks_b, w_out,
           norm_f2_pre, norm_f2_post, w_f2_gate, w_f2_up, w_f2_down,
           norm_ple_pre, norm_ple_post, w_ple_gate, w_ple_proj):
    assert x_sample.shape[1] == 1 and cache_a_k.shape[2] == WIN_A and cache_b_k.shape[2] == WIN_B
    norms = (norm_f1_pre, norm_f1_post, norm_mix_pre, norm_mix_post,
             norm_f2_pre, norm_f2_post, norm_ple_pre, norm_ple_post)
    weights = (w_f1_gate, w_f1_up, w_f1_down, w_in, w_out,
               w_f2_gate, w_f2_up, w_f2_down, w_ple_gate, w_ple_proj)
    caches = (cache_a_k, cache_a_v, cache_b_k, cache_b_v)
    hp, hs = x_prompt, x_sample
    per_layer = []
    for i in range(norm_f1_pre.shape[0]):
        hp, hs, new = _layer(i, hp, hs, caches, p_prompt, p_sample, norms, weights, sinks_b)
        per_layer.append(new)
    stacked = [jnp.stack([layer[j] for layer in per_layer]) for j in range(8)]
    return (hp, hs, *stacked)
```

```python
import functools

import jax
import jax.numpy as jnp
from jax import lax
from jax.experimental import pallas as pl
from jax.experimental.pallas import tpu as pltpu

F32 = jnp.float32
BF16 = jnp.bfloat16

D_MODEL = 1024
HEAD_DIM = 64
HALF = HEAD_DIM // 2
N_HEADS_A = 8
N_HEADS_B = 8
N_KV_B = 2
GROUP_B = N_HEADS_B // N_KV_B
DILATIONS = (1, 4, 16)
WIN_STEPS = 128
WIN_A = 2048
WIN_B = 128
PAST_LEN = 16384
D_FF = 2816
D_PLE = 256
ROPE_THETA = 10000.0
EPS = 1e-6
SCALE = HEAD_DIM ** -0.5
QA_W = N_HEADS_A * HEAD_DIM
QB_W = N_HEADS_B * HEAD_DIM
KVB_W = N_KV_B * HEAD_DIM
IN_W = 3 * QA_W + QB_W + 2 * KVB_W
LANES = 128
QBLK = 128

ROW_TILE = 128
ROW_BLOCK = 128
ATT_ROWS = 1024
ATT_BLOCKS = 16
VMEM_LIMIT = 56 * 1024 * 1024

K_PASS_COLS = 2 * N_HEADS_A
V_COL_QB, V_COL_KB, V_COL_VB = N_HEADS_A, N_HEADS_A + N_HEADS_B, N_HEADS_A + N_HEADS_B + N_KV_B
V_PASS_COLS = V_COL_VB + N_KV_B
OUT_COLS = N_HEADS_A + N_HEADS_B
STATS = LANES


def _whole(shape):
    nd = len(shape)
    return pl.BlockSpec(shape, lambda *_: (0,) * nd, pipeline_mode=pl.Buffered(1))


def _rmsnorm(x, g):
    return x * lax.rsqrt(jnp.mean(x * x, axis=-1, keepdims=True) + EPS) * g


def _mm(a, w):
    return jnp.dot(a, w, preferred_element_type=F32)


def _run_staggered(pipelines, extras=None):
    extras = extras or {}
    depth = len(pipelines[0])
    for t in range(depth + len(pipelines) - 1):
        for j, stages in enumerate(pipelines):
            if 0 <= t - j < depth:
                stages[t - j]()
        for piece in extras.get(t, ()):
            piece()


def _swiglu_stages(st, g_pre_ref, g_post_ref, wg_ref, wu_ref, wd_ref):
    def pre():
        st["u"] = _rmsnorm(st["h"], g_pre_ref[...]).astype(BF16)

    def gate_up():
        u = st.pop("u")
        st["gate"], st["up"] = _mm(u, wg_ref[...]), _mm(u, wu_ref[...])

    def activate():
        gate = st.pop("gate")
        st["act"] = (gate * jax.nn.sigmoid(gate) * st.pop("up")).astype(BF16)

    def down():
        st["y"] = _mm(st.pop("act"), wd_ref[...])

    def post():
        st["h"] = st["h"] + 0.5 * _rmsnorm(st.pop("y"), g_post_ref[...])

    return [pre, gate_up, activate, down, post]


def _merge_stages(first, second):
    last, head = first[-1], second[0]

    def both():
        last()
        head()

    return first[:-1] + [both] + second[1:]


def _rope(z, cos, sin_signed):
    rows, width = z.shape
    lane = lax.broadcasted_iota(jnp.int32, (rows, LANES), 1)
    first_half = (lane & HALF) == 0
    outs = []
    for c in range(width // LANES):
        zc = z[:, c * LANES:(c + 1) * LANES]
        partner = jnp.where(first_half,
                            pltpu.roll(zc, LANES - HALF, 1),
                            pltpu.roll(zc, HALF, 1))
        outs.append(zc * cos + partner * sin_signed)
    return outs[0] if len(outs) == 1 else jnp.concatenate(outs, axis=1)


_IN_WIDTHS = (QA_W, QA_W, QA_W, QB_W, KVB_W, KVB_W)


def _project_stages(st, rows, g_pre_ref, win_ref, cos_ref, sin_ref, emit):
    def pre():
        st["u"] = _rmsnorm(st["h"], g_pre_ref[...]).astype(BF16)

    def project():
        u, lo, pieces = st.pop("u"), 0, []
        for width in _IN_WIDTHS:
            pieces.append(_mm(u, win_ref[:, lo:lo + width]))
            lo += width
        st["z"] = pieces

    def rotate():
        qa, ka, va, qb, kb, vb = st.pop("z")
        cos, sin = cos_ref[rows, :], sin_ref[rows, :]
        emit(_rope(qa, cos, sin) * SCALE, _rope(ka, cos, sin), va,
             _rope(qb, cos, sin) * SCALE, _rope(kb, cos, sin), vb)

    return [pre, project, rotate]


def _stage1_stages(st, rows, x_ref, cos_ref, sin_ref, g1a_ref, g1b_ref, gmix_ref,
                   wg_ref, wu_ref, wd_ref, win_ref, h_ref, emit):
    def load():
        st["h"] = x_ref[rows, :]

    def store():
        h_ref[rows, :] = st["h"]

    ffn = _swiglu_stages(st, g1a_ref, g1b_ref, wg_ref, wu_ref, wd_ref)
    proj = _project_stages(st, rows, gmix_ref, win_ref, cos_ref, sin_ref, emit)
    return _merge_stages(_merge_stages([load], ffn), _merge_stages([store], proj))


def _dup_heads(x):
    lane = lax.broadcasted_iota(jnp.int32, x.shape, 1)
    swapped = pltpu.roll(x, HEAD_DIM, 1)
    lo = lane < HEAD_DIM
    return jnp.where(lo, x, swapped), jnp.where(lo, swapped, x)


def _slide_window(buf_t, new_col):
    length = buf_t.shape[1]
    lane = lax.broadcasted_iota(jnp.int32, buf_t.shape, 1)
    return jnp.where(lane == length - 1, new_col, pltpu.roll(buf_t, length - 1, 1))


def _key_pass_pieces(cols_ref, ck_ref, w_ref, nak_ref):
    heads, _, la = ck_ref.shape
    shared = {}

    def pattern_count():
        if "count" not in shared:
            dist = la - lax.broadcasted_iota(jnp.int32, (1, la), 1)
            count = jnp.zeros((1, la), F32)
            for d in DILATIONS:
                in_pattern = jnp.logical_and((dist & (d - 1)) == 0, dist <= d * WIN_STEPS)
                count = count + in_pattern.astype(F32)
            shared["count"] = count
        return shared["count"]

    def piece(h):
        def run():
            count = pattern_count()
            q = cols_ref[:, h:h + 1]
            k_new = cols_ref[:, heads + h:heads + h + 1]
            k_t = ck_ref[h]
            s = jnp.sum(q * k_t, axis=0, keepdims=True)
            s = jnp.where(count > 0.0, s, -jnp.inf)
            s_new = jnp.sum(q * k_new, axis=0, keepdims=True)
            m = jnp.maximum(jnp.max(s, axis=1, keepdims=True), s_new)
            e = jnp.exp(s - m) * count
            e_new = float(len(DILATIONS)) * jnp.exp(s_new - m)
            den = jnp.sum(e, axis=1, keepdims=True) + e_new
            w_ref[h:h + 1, 0:la] = e
            lane = lax.broadcasted_iota(jnp.int32, (1, STATS), 1)
            w_ref[h:h + 1, la:la + STATS] = jnp.where(lane < STATS // 2, e_new, den)
            nak_ref[h] = _slide_window(k_t, k_new)

        return run

    return [piece(h) for h in range(heads)]


def _value_pass_pieces(sink_ref, cols_ref, w_ref, cv_ref, cbk_ref, cbv_ref,
                       out_ref, nav_ref, nbk_ref, nbv_ref):
    heads, _, la = cv_ref.shape

    def piece(h):
        def run():
            e = w_ref[h:h + 1, 0:la]
            e_new = w_ref[h:h + 1, la:la + 1]
            den = w_ref[h:h + 1, la + STATS // 2:la + STATS // 2 + 1]
            v_new = cols_ref[:, h:h + 1]
            v_t = cv_ref[h]
            acc = jnp.sum(v_t * e, axis=1, keepdims=True) + e_new * v_new
            out_ref[:, h:h + 1] = acc / den
            nav_ref[h] = _slide_window(v_t, v_new)

        return run

    def group_b():
        gidx = lax.broadcasted_iota(jnp.int32, (GROUP_B, 1, 1), 0)
        for kv in range(N_KV_B):
            k_t, v_t = cbk_ref[kv], cbv_ref[kv]
            k_new = cols_ref[:, V_COL_KB + kv:V_COL_KB + kv + 1]
            v_new = cols_ref[:, V_COL_VB + kv:V_COL_VB + kv + 1]
            first = V_COL_QB + kv * GROUP_B
            q = jnp.stack([cols_ref[:, first + g:first + g + 1] for g in range(GROUP_B)])
            sink = jnp.zeros((GROUP_B, 1, 1), F32)
            for g in range(GROUP_B):
                sink = jnp.where(gidx == g, sink_ref[kv * GROUP_B + g], sink)
            s = jnp.sum(q * k_t[None], axis=1, keepdims=True)
            s_new = jnp.sum(q * k_new[None], axis=1, keepdims=True)
            m = jnp.maximum(jnp.maximum(jnp.max(s, axis=2, keepdims=True), s_new), sink)
            e = jnp.exp(s - m)
            e_new = jnp.exp(s_new - m)
            den = jnp.sum(e, axis=2, keepdims=True) + e_new + jnp.exp(sink - m)
            out = (jnp.sum(v_t[None] * e, axis=2, keepdims=True) + e_new * v_new[None]) / den
            for g in range(GROUP_B):
                lane = N_HEADS_A + kv * GROUP_B + g
                out_ref[:, lane:lane + 1] = out[g]
            nbk_ref[kv] = _slide_window(k_t, k_new)
            nbv_ref[kv] = _slide_window(v_t, v_new)

    return [group_b] + [piece(h) for h in range(heads)]


def _columns(pieces):
    n = pieces[0].shape[0]
    return jnp.concatenate(pieces, axis=1).reshape(n, -1, HEAD_DIM).transpose(0, 2, 1)


def _stage1_prompt_kernel(x_ref, cos_ref, sin_ref, g1a_ref, g1b_ref, gmix_ref,
                          wg_ref, wu_ref, wd_ref, win_ref, cols_ref, ck_ref,
                          h_ref, qa1_ref, ka1_ref, va1_ref, qa4_ref, ka4_ref, va4_ref,
                          qa16_ref, ka16_ref, va16_ref, qb_ref, kb2_ref, vb2_ref,
                          nak_ref, nav_ref, nbk_ref, nbv_ref, w_ref, sak_ref, stage_ref):
    tm = x_ref.shape[0]
    bm = min(ROW_BLOCK, tm)
    nblocks = tm // bm

    def emitter(j):
        rows = slice(j * bm, (j + 1) * bm)

        def decimated(z, nat_ref, dec_refs):
            nat_ref[rows, :] = z.astype(BF16)
            for c in range(QA_W // LANES):
                stage_ref[j, c] = z[:, c * LANES:(c + 1) * LANES]
            for d, ref in dec_refs:
                n = bm // d
                for r in range(d):
                    for c in range(QA_W // LANES):
                        lo = r * QA_W + c * LANES
                        piece = stage_ref[j, c, pl.ds(r, n, stride=d), :]
                        ref[j * n:(j + 1) * n, lo:lo + LANES] = piece.astype(BF16)

        def emit(qa, ka, va, qb, kb, vb):
            decimated(qa, qa1_ref, ((4, qa4_ref), (16, qa16_ref)))
            decimated(ka, ka1_ref, ((4, ka4_ref), (16, ka16_ref)))
            decimated(va, va1_ref, ((4, va4_ref), (16, va16_ref)))
            qb_ref[rows, :] = qb.astype(BF16)
            k0, k1 = _dup_heads(kb)
            kb2_ref[rows, 0:LANES] = k0.astype(BF16)
            kb2_ref[rows, LANES:2 * LANES] = k1.astype(BF16)
            v0, v1 = _dup_heads(vb)
            vb2_ref[rows, 0:LANES] = v0.astype(BF16)
            vb2_ref[rows, LANES:2 * LANES] = v1.astype(BF16)
            nak_ref[rows, :] = ka
            nav_ref[rows, :] = va
            if j == nblocks - 1:
                nbk_ref[...] = kb[bm - WIN_B:, :]
                nbv_ref[...] = vb[bm - WIN_B:, :]

        return rows, emit

    pipelines = []
    for j in range(nblocks):
        rows, emit = emitter(j)
        pipelines.append(_stage1_stages({}, rows, x_ref, cos_ref, sin_ref, g1a_ref, g1b_ref, gmix_ref,
                                        wg_ref, wu_ref, wd_ref, win_ref, h_ref, emit))
    pieces = _key_pass_pieces(cols_ref, ck_ref, w_ref, sak_ref)
    _run_staggered(pipelines, {0: pieces})


def _stage1_sample_kernel(x_ref, cos_ref, sin_ref, g1a_ref, g1b_ref, gmix_ref,
                          wg_ref, wu_ref, wd_ref, win_ref,
                          h_ref, qa_ref, ka_ref, va_ref, qb_ref, kb_ref, vb_ref):
    def emit(*pieces):
        for ref, z in zip((qa_ref, ka_ref, va_ref, qb_ref, kb_ref, vb_ref), pieces):
            ref[...] = z

    _run_staggered([_stage1_stages({}, slice(None), x_ref, cos_ref, sin_ref, g1a_ref, g1b_ref,
                                   gmix_ref, wg_ref, wu_ref, wd_ref, win_ref, h_ref, emit)])


def _stage1_weight_specs():
    return [_whole((1, D_MODEL))] * 3 + [
        _whole((D_MODEL, D_FF)), _whole((D_MODEL, D_FF)), _whole((D_FF, D_MODEL)),
        _whole((D_MODEL, IN_W))]


def _per_sequence_spec(tiles_per_batch, *block):
    return pl.BlockSpec((None,) + block,
                        lambda b, i: (b * tiles_per_batch + i,) + (0,) * len(block))


def _stage1_prompt(x, cos, sin, g1a, g1b, gmix, wg, wu, wd, win, cols_k, cak_t):
    nb, seq, _ = x.shape
    tm = ROW_TILE
    nt = seq // tm
    n, heads, _, la = cak_t.shape
    assert nb * nt == n and tm >= WIN_B
    first_kept = (seq - WIN_A) // tm

    def rows(width, dtype, d=1):
        shape = jax.ShapeDtypeStruct((nb, seq // d, d * width), dtype)
        spec = pl.BlockSpec((None, tm // d, d * width), lambda b, i: (b, i, 0))
        return shape, spec

    outs = [rows(D_MODEL, F32)]
    outs += [rows(QA_W, BF16)] * 3 + [rows(QA_W, BF16, 4)] * 3 + [rows(QA_W, BF16, 16)] * 3
    outs += [rows(QB_W, BF16), rows(2 * LANES, BF16), rows(2 * LANES, BF16)]
    kept = (jax.ShapeDtypeStruct((nb, WIN_A, QA_W), F32),
            pl.BlockSpec((None, tm, QA_W), lambda b, i: (b, jnp.maximum(i - first_kept, 0), 0)))
    last = (jax.ShapeDtypeStruct((nb, WIN_B, KVB_W), F32),
            pl.BlockSpec((None, WIN_B, KVB_W), lambda b, i: (b, 0, 0)))
    outs += [kept, kept, last, last]
    outs += [(jax.ShapeDtypeStruct((n, heads, la + STATS), F32), _per_sequence_spec(nt, heads, la + STATS)),
             (jax.ShapeDtypeStruct(cak_t.shape, F32), _per_sequence_spec(nt, heads, HEAD_DIM, la))]
    in_specs = ([pl.BlockSpec((None, tm, D_MODEL), lambda b, i: (b, i, 0)),
                 pl.BlockSpec((tm, LANES), lambda b, i: (i, 0)),
                 pl.BlockSpec((tm, LANES), lambda b, i: (i, 0))] + _stage1_weight_specs()
                + [_per_sequence_spec(nt, HEAD_DIM, K_PASS_COLS),
                   _per_sequence_spec(nt, heads, HEAD_DIM, la)])
    return pl.pallas_call(
        _stage1_prompt_kernel,
        out_shape=[o[0] for o in outs],
        grid=(nb, nt),
        in_specs=in_specs,
        out_specs=[o[1] for o in outs],
        scratch_shapes=[pltpu.VMEM((tm // ROW_BLOCK, QA_W // LANES, ROW_BLOCK, LANES), F32)],
        compiler_params=pltpu.CompilerParams(
            dimension_semantics=("arbitrary", "arbitrary"), vmem_limit_bytes=VMEM_LIMIT),
        name="stage1_prompt",
    )(x, cos, sin, g1a, g1b, gmix, wg, wu, wd, win, cols_k, cak_t)


def _stage1_sample(x, cos, sin, g1a, g1b, gmix, wg, wu, wd, win):
    n = x.shape[0]
    widths = (D_MODEL, QA_W, QA_W, QA_W, QB_W, KVB_W, KVB_W)
    return pl.pallas_call(
        _stage1_sample_kernel,
        out_shape=[jax.ShapeDtypeStruct((n, w), F32) for w in widths],
        grid=(1,),
        in_specs=[_whole((n, D_MODEL)), _whole((n, LANES)), _whole((n, LANES))]
        + _stage1_weight_specs(),
        out_specs=[pl.BlockSpec((n, w), lambda i: (0, 0)) for w in widths],
        compiler_params=pltpu.CompilerParams(
            dimension_semantics=("arbitrary",), vmem_limit_bytes=VMEM_LIMIT),
        name="stage1_sample",
    )(x, cos, sin, g1a, g1b, gmix, wg, wu, wd, win)


def _banded_head_pair(q, k_prev, k_cur, v_prev, v_cur, has_prev, sinks):
    rows = q.shape[0]
    nblk = rows // QBLK
    keys = 2 * QBLK

    low_lanes = lax.broadcasted_iota(jnp.int32, (rows, LANES), 1) < HEAD_DIM
    zero = jnp.zeros_like(q)
    q2 = jnp.concatenate([jnp.where(low_lanes, q, zero).reshape(nblk, QBLK, LANES),
                          jnp.where(low_lanes, zero, q).reshape(nblk, QBLK, LANES)], axis=1)
    k_all = jnp.concatenate([k_prev, k_cur], axis=0)
    v_all = jnp.concatenate([v_prev, v_cur], axis=0)
    k2 = jnp.stack([k_all[b * QBLK:b * QBLK + keys] for b in range(nblk)])
    ones = jnp.ones((keys, LANES), BF16)
    v2 = jnp.stack([jnp.concatenate([v_all[b * QBLK:b * QBLK + keys], ones], axis=1)
                    for b in range(nblk)])

    s = jnp.einsum("bqd,bkd->bqk", q2, k2, preferred_element_type=F32)
    qrow = lax.broadcasted_iota(jnp.int32, (keys, keys), 0) & (QBLK - 1)
    col = lax.broadcasted_iota(jnp.int32, (keys, keys), 1)
    in_band = jnp.logical_and(col >= qrow, col <= qrow + QBLK)
    s = jnp.where(in_band[None], s, -jnp.inf)
    first_ok = jnp.logical_or(has_prev, col >= QBLK)
    s = jnp.concatenate([jnp.where(first_ok[None], s[0:1], -jnp.inf), s[1:]], axis=0)
    m = jnp.max(s, axis=2, keepdims=True)
    if sinks is not None:
        head1 = lax.broadcasted_iota(jnp.int32, (keys, 1), 0) >= QBLK
        sink = jnp.where(head1, sinks[1], sinks[0])[None]
        m = jnp.maximum(m, sink)
    e = jnp.exp(s - m).astype(BF16)
    pv = jnp.einsum("bqk,bkd->bqd", e, v2, preferred_element_type=F32)
    acc, den = pv[:, :, :LANES], pv[:, :, LANES:]
    if sinks is not None:
        den = den + jnp.exp(sink - m)
    out2 = acc / den
    low3 = low_lanes.reshape(nblk, QBLK, LANES)
    out = jnp.where(low3, out2[:, :QBLK], out2[:, QBLK:]).reshape(rows, LANES)
    lse2 = m + jnp.log(den)
    lse = jnp.where(low3, lse2[:, :QBLK], lse2[:, QBLK:]).reshape(rows, LANES)
    return out, lse


def _banded_attention_kernel(*refs, with_sink, with_lse, shared_kv):
    refs = list(refs)
    sink_ref = refs.pop(0) if with_sink else None
    q_ref, kp_ref, kc_ref, vp_ref, vc_ref, o_ref = refs[:6]
    lse_ref = refs[6] if with_lse else None
    pairs = q_ref.shape[1] // LANES
    has_prev = pl.program_id(2) > 0
    for g in range(pairs):
        lanes = slice(g * LANES, (g + 1) * LANES)
        kg = g // shared_kv
        kv_lanes = slice(kg * LANES, (kg + 1) * LANES)
        sinks = None
        if with_sink:
            pair = pl.program_id(1) * pairs + g
            sinks = (sink_ref[2 * pair], sink_ref[2 * pair + 1])
        out, lse = _banded_head_pair(q_ref[:, lanes], kp_ref[:, kv_lanes], kc_ref[:, kv_lanes],
                                     vp_ref[:, kv_lanes], vc_ref[:, kv_lanes], has_prev, sinks)
        o_ref[:, lanes] = out.astype(o_ref.dtype)
        if with_lse:
            lse_ref[:, lanes] = lse


def _banded_attention(q, k, v, *, shared_kv=1, sinks=None, with_lse, name):
    nb, length, width = q.shape
    rows = min(ATT_ROWS, length)
    sub = rows // QBLK
    pairs = max(shared_kv, min(ATT_BLOCKS // sub, width // LANES))
    assert pairs % shared_kv == 0 and (width // LANES) % pairs == 0
    grid = (nb, width // (pairs * LANES), length // rows)
    kv_lanes = pairs * LANES // shared_kv
    q_spec = pl.BlockSpec((None, rows, pairs * LANES), lambda b, c, i: (b, i, c))
    cur_spec = pl.BlockSpec((None, rows, kv_lanes), lambda b, c, i: (b, i, c))
    prev_spec = pl.BlockSpec((None, QBLK, kv_lanes),
                             lambda b, c, i: (b, jnp.maximum(i * sub - 1, 0), c))
    in_specs = [q_spec, prev_spec, cur_spec, prev_spec, cur_spec]
    args = [q, k, k, v, v]
    if sinks is not None:
        in_specs = [pl.BlockSpec(memory_space=pltpu.SMEM)] + in_specs
        args = [sinks] + args
    out_shape = [jax.ShapeDtypeStruct(q.shape, BF16)]
    out_specs = [q_spec]
    if with_lse:
        out_shape.append(jax.ShapeDtypeStruct(q.shape, F32))
        out_specs.append(q_spec)
    return pl.pallas_call(
        functools.partial(_banded_attention_kernel, with_sink=sinks is not None, with_lse=with_lse,
                          shared_kv=shared_kv),
        out_shape=out_shape,
        grid=grid,
        in_specs=in_specs,
        out_specs=out_specs,
        compiler_params=pltpu.CompilerParams(
            dimension_semantics=("arbitrary",) * 3, vmem_limit_bytes=VMEM_LIMIT),
        name=name,
    )(*args)


def _stage3_stages(st, rows, mixed, h_ref, p_ref, norm_refs, weight_refs, out_ref):
    gmix_ref, g2a_ref, g2b_ref, gpa_ref, gpb_ref = norm_refs
    wout_ref, wg_ref, wu_ref, wd_ref, wpg_ref, wpp_ref = weight_refs

    def load():
        st["oa"], st["ob"] = mixed()

    def out_proj():
        st["y"] = (_mm(st.pop("oa"), wout_ref[0:QA_W, :])
                   + _mm(st.pop("ob"), wout_ref[QA_W:QA_W + QB_W, :]))

    def mix_residual():
        st["h"] = h_ref[rows, :] + _rmsnorm(st.pop("y"), gmix_ref[...])

    def ple_pre():
        st["u"] = _rmsnorm(st["h"], gpa_ref[...]).astype(BF16)
        st["p"] = p_ref[rows, :].astype(BF16)

    def ple_dots():
        st["gate"], st["proj"] = _mm(st.pop("u"), wpg_ref[...]), _mm(st.pop("p"), wpp_ref[...])

    def ple_residual():
        y = jax.nn.sigmoid(st.pop("gate")) * st.pop("proj")
        out_ref[rows, :] = st.pop("h") + _rmsnorm(y, gpb_ref[...])

    ffn = _swiglu_stages(st, g2a_ref, g2b_ref, wg_ref, wu_ref, wd_ref)
    head = _merge_stages([load, out_proj, mix_residual], ffn)
    return _merge_stages(head, [ple_pre, ple_dots, ple_residual])


def _stage3_prompt_kernel(sink_ref, h_ref, o1_ref, l1_ref, o4_ref, l4_ref, o16_ref, l16_ref,
                          ob_ref, p_ref, *rest):
    norm_refs, weight_refs = rest[:5], rest[5:11]
    cols_ref, w_ref, cv_ref, cbk_ref, cbv_ref = rest[11:16]
    out_ref, so_ref, sav_ref, sbk_ref, sbv_ref = rest[16:21]
    o4s_ref, l4s_ref, o16s_ref, l16s_ref = rest[21:]
    tm = h_ref.shape[0]
    bm = min(ROW_BLOCK, tm)

    def mixer(j):
        rows = slice(j * bm, (j + 1) * bm)

        def mixed():
            for d, src, dst in ((4, o4_ref, o4s_ref), (4, l4_ref, l4s_ref),
                                (16, o16_ref, o16s_ref), (16, l16_ref, l16s_ref)):
                n = bm // d
                for r in range(d):
                    for c in range(QA_W // LANES):
                        lo = r * QA_W + c * LANES
                        piece = src[j * n:(j + 1) * n, lo:lo + LANES].astype(F32)
                        dst[j, c, pl.ds(r, n, stride=d), :] = piece
            pieces = []
            for c in range(QA_W // LANES):
                cols = slice(c * LANES, (c + 1) * LANES)
                l1, l4, l16 = l1_ref[rows, cols], l4s_ref[j, c], l16s_ref[j, c]
                m = jnp.maximum(l1, jnp.maximum(l4, l16))
                w1, w4, w16 = jnp.exp(l1 - m), jnp.exp(l4 - m), jnp.exp(l16 - m)
                mix = w1 * o1_ref[rows, cols].astype(F32) + w4 * o4s_ref[j, c] + w16 * o16s_ref[j, c]
                pieces.append(mix / (w1 + w4 + w16))
            return jnp.concatenate(pieces, axis=1).astype(BF16), ob_ref[rows, :]

        return rows, mixed

    pipelines = []
    for j in range(tm // bm):
        rows, mixed = mixer(j)
        pipelines.append(_stage3_stages({}, rows, mixed, h_ref, p_ref, norm_refs, weight_refs, out_ref))
    pieces = _value_pass_pieces(sink_ref, cols_ref, w_ref, cv_ref, cbk_ref, cbv_ref,
                                so_ref, sav_ref, sbk_ref, sbv_ref)
    _run_staggered(pipelines, {0: pieces})


def _stage3_sample_kernel(h_ref, o_ref, p_ref, *rest):
    norm_refs, weight_refs, out_ref = rest[:5], rest[5:11], rest[11]
    mixed = lambda: (o_ref[:, 0:QA_W].astype(BF16), o_ref[:, QA_W:QA_W + QB_W].astype(BF16))
    _run_staggered([_stage3_stages({}, slice(None), mixed, h_ref, p_ref, norm_refs, weight_refs,
                                   out_ref)])


def _stage3_weight_specs():
    return [_whole((1, D_MODEL))] * 5 + [
        _whole((QA_W + QB_W, D_MODEL)), _whole((D_MODEL, D_FF)), _whole((D_MODEL, D_FF)),
        _whole((D_FF, D_MODEL)), _whole((D_MODEL, D_MODEL)), _whole((D_PLE, D_MODEL))]


def _stage3_prompt(sinks, h, o1, l1, o4, l4, o16, l16, ob, p, norms, weights,
                   cols_v, w, cav_t, cbk_t, cbv_t):
    nb, seq, _ = h.shape
    tm = ROW_TILE
    nt = seq // tm
    n, heads, _, la = cav_t.shape
    lb = cbk_t.shape[-1]
    assert nb * nt == n

    def rows(width, d=1):
        return pl.BlockSpec((None, tm // d, d * width), lambda b, i: (b, i, 0))

    in_specs = ([pl.BlockSpec(memory_space=pltpu.SMEM),
                 rows(D_MODEL), rows(QA_W), rows(QA_W), rows(QA_W, 4), rows(QA_W, 4),
                 rows(QA_W, 16), rows(QA_W, 16), rows(QB_W), rows(D_PLE)] + _stage3_weight_specs()
                + [_per_sequence_spec(nt, HEAD_DIM, V_PASS_COLS),
                   _per_sequence_spec(nt, heads, la + STATS),
                   _per_sequence_spec(nt, heads, HEAD_DIM, la),
                   _per_sequence_spec(nt, N_KV_B, HEAD_DIM, lb),
                   _per_sequence_spec(nt, N_KV_B, HEAD_DIM, lb)])
    out_shape = [jax.ShapeDtypeStruct(h.shape, F32),
                 jax.ShapeDtypeStruct((n, HEAD_DIM, OUT_COLS), F32),
                 jax.ShapeDtypeStruct(cav_t.shape, F32),
                 jax.ShapeDtypeStruct(cbk_t.shape, F32), jax.ShapeDtypeStruct(cbv_t.shape, F32)]
    out_specs = [rows(D_MODEL), _per_sequence_spec(nt, HEAD_DIM, OUT_COLS),
                 _per_sequence_spec(nt, heads, HEAD_DIM, la),
                 _per_sequence_spec(nt, N_KV_B, HEAD_DIM, lb),
                 _per_sequence_spec(nt, N_KV_B, HEAD_DIM, lb)]
    return pl.pallas_call(
        _stage3_prompt_kernel,
        out_shape=out_shape,
        grid=(nb, nt),
        in_specs=in_specs,
        out_specs=out_specs,
        scratch_shapes=[pltpu.VMEM((tm // ROW_BLOCK, QA_W // LANES, ROW_BLOCK, LANES), F32)] * 4,
        compiler_params=pltpu.CompilerParams(
            dimension_semantics=("arbitrary", "arbitrary"), vmem_limit_bytes=VMEM_LIMIT),
        name="stage3_prompt",
    )(sinks, h, o1, l1, o4, l4, o16, l16, ob, p, *norms, *weights, cols_v, w, cav_t, cbk_t, cbv_t)


def _stage3_sample(h, o, p, norms, weights):
    n = h.shape[0]
    return pl.pallas_call(
        _stage3_sample_kernel,
        out_shape=jax.ShapeDtypeStruct(h.shape, F32),
        grid=(1,),
        in_specs=[_whole((n, D_MODEL)), _whole((n, QA_W + QB_W)), _whole((n, D_PLE))]
        + _stage3_weight_specs(),
        out_specs=pl.BlockSpec((n, D_MODEL), lambda i: (0, 0)),
        compiler_params=pltpu.CompilerParams(
            dimension_semantics=("arbitrary",), vmem_limit_bytes=VMEM_LIMIT),
        name="stage3_sample",
    )(h, o, p, *norms, *weights)


def _rope_tables(pos):
    inv = jnp.power(ROPE_THETA, -jnp.arange(HALF, dtype=F32) * 2.0 / HEAD_DIM)
    ang = pos.astype(F32)[:, None] * inv[None, :]
    c, s = jnp.cos(ang), jnp.sin(ang)
    return jnp.concatenate([c, c, c, c], axis=-1), jnp.concatenate([-s, s, -s, s], axis=-1)


def _layer(i, hp, hs, caches, p_prompt, p_sample, norms, weights, sinks):
    (g1a, g1b, gmix_a, gmix_b, g2a, g2b, gpa, gpb) = [g[i][None, :] for g in norms]
    (wg1, wu1, wd1, win, wout, wg2, wu2, wd2, wpg, wpp) = [w[i].astype(BF16) for w in weights]
    rows_last = lambda z: z.transpose(0, 2, 3, 1)
    rows_first = lambda z: z.transpose(0, 3, 1, 2)
    cak_t, cav_t, cbk_t, cbv_t = [rows_last(c[i]) for c in caches]
    nb, seq, _ = hp.shape
    n_dec, dec_seq, _ = hs.shape
    sink = sinks[i].astype(F32)

    cos_p, sin_p = _rope_tables(jnp.arange(seq, dtype=jnp.int32))
    pos_s = jnp.broadcast_to(PAST_LEN + jnp.arange(dec_seq, dtype=jnp.int32)[None, :],
                             (n_dec, dec_seq)).reshape(-1)
    cos_s, sin_s = _rope_tables(pos_s)
    stage1_w = (g1a, g1b, gmix_a, wg1, wu1, wd1, win)
    stage3_n = (gmix_b, g2a, g2b, gpa, gpb)
    stage3_w = (wout, wg2, wu2, wd2, wpg, wpp)

    xs = hs.reshape(n_dec * dec_seq, D_MODEL)
    h1s, qa_s, ka_s, va_s, qb_s, kb_s, vb_s = _stage1_sample(xs, cos_s, sin_s, *stage1_w)
    cols_k = _columns([qa_s, ka_s])
    cols_v = _columns([va_s, qb_s, kb_s, vb_s])

    (h1, qa1, ka1, va1, qa4, ka4, va4, qa16, ka16, va16, qb, kb2, vb2,
     nak_p, nav_p, nbk_p, nbv_p, w_s, nak_s) = _stage1_prompt(hp, cos_p, sin_p, *stage1_w,
                                                             cols_k, cak_t)
    o1, l1 = _banded_attention(qa1, ka1, va1, with_lse=True, name="dilated_d1")
    o4, l4 = _banded_attention(qa4, ka4, va4, with_lse=True, name="dilated_d4")
    o16, l16 = _banded_attention(qa16, ka16, va16, with_lse=True, name="dilated_d16")
    (ob,) = _banded_attention(qb, kb2, vb2, shared_kv=GROUP_B // 2, sinks=sink,
                              with_lse=False, name="swa_sink")
    hp, o_cols, nav_s, nbk_s, nbv_s = _stage3_prompt(
        sink, h1, o1, l1, o4, l4, o16, l16, ob, p_prompt[i], stage3_n, stage3_w,
        cols_v, w_s, cav_t, cbk_t, cbv_t)

    o_s = o_cols.transpose(0, 2, 1).reshape(n_dec, QA_W + QB_W)
    hs = _stage3_sample(h1s, o_s, p_sample[i].reshape(n_dec * dec_seq, D_PLE), stage3_n, stage3_w)
    hs = hs.reshape(n_dec, dec_seq, D_MODEL)

    heads5 = lambda z, h: z.reshape(z.shape[0], z.shape[1], h, HEAD_DIM)
    new = (heads5(nak_p, N_HEADS_A), heads5(nav_p, N_HEADS_A), heads5(nbk_p, N_KV_B),
           heads5(nbv_p, N_KV_B), rows_first(nak_s), rows_first(nav_s), rows_first(nbk_s),
           rows_first(nbv_s))
    return hp, hs, new


def kernel(x_prompt, x_sample, cache_a_k, cache_a_v, cache_b_k, cache_b_v, p_prompt, p_sample,
           norm_f1_pre, norm_f1_post, w_f1_gate, w_f1_up, w_f1_down,
           norm_mix_pre, norm_mix_post, w_in, sinks_b, w_out,
           norm_f2_pre, norm_f2_post, w_f2_gate, w_f2_up, w_f2_down,
           norm_ple_pre, norm_ple_post, w_ple_gate, w_ple_proj):
    assert x_sample.shape[1] == 1 and cache_a_k.shape[2] == WIN_A and cache_b_k.shape[2] == WIN_B
    norms = (norm_f1_pre, norm_f1_post, norm_mix_pre, norm_mix_post,
             norm_f2_pre, norm_f2_post, norm_ple_pre, norm_ple_post)
    weights = (w_f1_gate, w_f1_up, w_f1_down, w_in, w_out,
               w_f2_gate, w_f2_up, w_f2_down, w_ple_gate, w_ple_proj)
    caches = (cache_a_k, cache_a_v, cache_b_k, cache_b_v)
    hp, hs = x_prompt, x_sample
    per_layer = []
    for i in range(norm_f1_pre.shape[0]):
        hp, hs, new = _layer(i, hp, hs, caches, p_prompt, p_sample, norms, weights, sinks_b)
        per_layer.append(new)
    stacked = [jnp.stack([layer[j] for layer in per_layer]) for j in range(8)]
    return (hp, hs, *stacked)
```

```python
import functools

import jax
import jax.numpy as jnp
from jax import lax
from jax.experimental import pallas as pl
from jax.experimental.pallas import tpu as pltpu

F32 = jnp.float32
BF16 = jnp.bfloat16

D_MODEL = 1024
HEAD_DIM = 64
HALF = HEAD_DIM // 2
N_HEADS_A = 8
N_HEADS_B = 8
N_KV_B = 2
GROUP_B = N_HEADS_B // N_KV_B
DILATIONS = (1, 4, 16)
WIN_STEPS = 128
WIN_A = 2048
WIN_B = 128
PAST_LEN = 16384
D_FF = 2816
D_PLE = 256
ROPE_THETA = 10000.0
EPS = 1e-6
SCALE = HEAD_DIM ** -0.5
QA_W = N_HEADS_A * HEAD_DIM
QB_W = N_HEADS_B * HEAD_DIM
KVB_W = N_KV_B * HEAD_DIM
IN_W = 3 * QA_W + QB_W + 2 * KVB_W
LANES = 128
QBLK = 128

HOST_TILE = 128
ROW_TILE = 512
ROW_BLOCK = 256
ATT_ROWS = 1024
ATT_BLOCKS = 16
VMEM_LIMIT = 56 * 1024 * 1024

K_PASS_COLS = 2 * N_HEADS_A
V_COL_QB, V_COL_KB, V_COL_VB = N_HEADS_A, N_HEADS_A + N_HEADS_B, N_HEADS_A + N_HEADS_B + N_KV_B
V_PASS_COLS = V_COL_VB + N_KV_B
OUT_COLS = N_HEADS_A + N_HEADS_B
STATS = LANES


def _whole(shape):
    nd = len(shape)
    return pl.BlockSpec(shape, lambda *_: (0,) * nd, pipeline_mode=pl.Buffered(1))


def _rmsnorm(x, g):
    return x * lax.rsqrt(jnp.mean(x * x, axis=-1, keepdims=True) + EPS) * g


def _mm(a, w):
    return jnp.dot(a, w, preferred_element_type=F32)


def _run_staggered(pipelines, extras=None):
    extras = extras or {}
    depth = len(pipelines[0])
    for t in range(depth + len(pipelines) - 1):
        for j, stages in enumerate(pipelines):
            if 0 <= t - j < depth:
                stages[t - j]()
        for piece in extras.get(t, ()):
            piece()


def _swiglu_stages(st, g_pre_ref, g_post_ref, wg_ref, wu_ref, wd_ref):
    def pre():
        st["u"] = _rmsnorm(st["h"], g_pre_ref[...]).astype(BF16)

    def gate_up():
        u = st.pop("u")
        st["gate"], st["up"] = _mm(u, wg_ref[...]), _mm(u, wu_ref[...])

    def activate():
        gate = st.pop("gate")
        st["act"] = (gate * jax.nn.sigmoid(gate) * st.pop("up")).astype(BF16)

    def down():
        st["y"] = _mm(st.pop("act"), wd_ref[...])

    def post():
        st["h"] = st["h"] + 0.5 * _rmsnorm(st.pop("y"), g_post_ref[...])

    return [pre, gate_up, activate, down, post]


def _merge_stages(first, second):
    last, head = first[-1], second[0]

    def both():
        last()
        head()

    return first[:-1] + [both] + second[1:]


def _rope(z, cos, sin_signed):
    rows, width = z.shape
    lane = lax.broadcasted_iota(jnp.int32, (rows, LANES), 1)
    first_half = (lane & HALF) == 0
    outs = []
    for c in range(width // LANES):
        zc = z[:, c * LANES:(c + 1) * LANES]
        partner = jnp.where(first_half,
                            pltpu.roll(zc, LANES - HALF, 1),
                            pltpu.roll(zc, HALF, 1))
        outs.append(zc * cos + partner * sin_signed)
    return outs[0] if len(outs) == 1 else jnp.concatenate(outs, axis=1)


_IN_WIDTHS = (QA_W, QA_W, QA_W, QB_W, KVB_W, KVB_W)


def _project_stages(st, rows, g_pre_ref, win_ref, cos_ref, sin_ref, emit):
    def pre():
        st["u"] = _rmsnorm(st["h"], g_pre_ref[...]).astype(BF16)

    def project():
        u, lo, pieces = st.pop("u"), 0, []
        for width in _IN_WIDTHS:
            pieces.append(_mm(u, win_ref[:, lo:lo + width]))
            lo += width
        st["z"] = pieces

    def rotate():
        qa, ka, va, qb, kb, vb = st.pop("z")
        cos, sin = cos_ref[rows, :], sin_ref[rows, :]
        emit(_rope(qa, cos, sin) * SCALE, _rope(ka, cos, sin), va,
             _rope(qb, cos, sin) * SCALE, _rope(kb, cos, sin), vb)

    return [pre, project, rotate]


def _stage1_stages(st, rows, x_ref, cos_ref, sin_ref, g1a_ref, g1b_ref, gmix_ref,
                   wg_ref, wu_ref, wd_ref, win_ref, h_ref, emit):
    def load():
        st["h"] = x_ref[rows, :]

    def store():
        h_ref[rows, :] = st["h"]

    ffn = _swiglu_stages(st, g1a_ref, g1b_ref, wg_ref, wu_ref, wd_ref)
    proj = _project_stages(st, rows, gmix_ref, win_ref, cos_ref, sin_ref, emit)
    return _merge_stages(_merge_stages([load], ffn), _merge_stages([store], proj))


def _dup_heads(x):
    lane = lax.broadcasted_iota(jnp.int32, x.shape, 1)
    swapped = pltpu.roll(x, HEAD_DIM, 1)
    lo = lane < HEAD_DIM
    return jnp.where(lo, x, swapped), jnp.where(lo, swapped, x)


def _slide_window(buf_t, new_col):
    length = buf_t.shape[1]
    lane = lax.broadcasted_iota(jnp.int32, buf_t.shape, 1)
    return jnp.where(lane == length - 1, new_col, pltpu.roll(buf_t, length - 1, 1))


def _key_pass_pieces(cols_ref, ck_ref, w_ref, nak_ref):
    heads, _, la = ck_ref.shape
    scores, new_scores = {}, {}

    def score(h):
        def run():
            q = cols_ref[:, h:h + 1]
            k_new = cols_ref[:, heads + h:heads + h + 1]
            scores[h] = jnp.sum(q * ck_ref[h], axis=0, keepdims=True)
            new_scores[h] = jnp.sum(q * k_new, axis=0, keepdims=True)

        return run

    def softmax():
        dist = la - lax.broadcasted_iota(jnp.int32, (1, la), 1)
        count = jnp.zeros((1, la), F32)
        for d in DILATIONS:
            in_pattern = jnp.logical_and((dist & (d - 1)) == 0, dist <= d * WIN_STEPS)
            count = count + in_pattern.astype(F32)
        s = jnp.concatenate([scores.pop(h) for h in range(heads)], axis=0)
        s_new = jnp.concatenate([new_scores.pop(h) for h in range(heads)], axis=0)
        s = jnp.where(count > 0.0, s, -jnp.inf)
        m = jnp.maximum(jnp.max(s, axis=1, keepdims=True), s_new)
        e = jnp.exp(s - m) * count
        e_new = float(len(DILATIONS)) * jnp.exp(s_new - m)
        den = jnp.sum(e, axis=1, keepdims=True) + e_new
        w_ref[:, 0:la] = e
        lane = lax.broadcasted_iota(jnp.int32, (heads, STATS), 1)
        w_ref[:, la:la + STATS] = jnp.where(lane < STATS // 2, e_new, den)

    def slide(h):
        def run():
            nak_ref[h] = _slide_window(ck_ref[h], cols_ref[:, heads + h:heads + h + 1])

        return run

    return [score(h) for h in range(heads)] + [softmax] + [slide(h) for h in range(heads)]


def _value_pass_pieces(sink_ref, cols_ref, w_ref, cv_ref, cbk_ref, cbv_ref,
                       out_ref, nav_ref, nbk_ref, nbv_ref):
    heads, _, la = cv_ref.shape
    partial = {}

    def accumulate(h):
        def run():
            weighted = cv_ref[h] * w_ref[h:h + 1, 0:la]
            acc = weighted[:, 0:LANES]
            for c in range(1, la // LANES):
                acc = acc + weighted[:, c * LANES:(c + 1) * LANES]
            partial[h] = acc

        return run

    def finalize():
        acc = jnp.stack([partial.pop(h) for h in range(heads)])
        e_new = jnp.stack([w_ref[h:h + 1, la:la + 1] for h in range(heads)])
        half = la + STATS // 2
        den = jnp.stack([w_ref[h:h + 1, half:half + 1] for h in range(heads)])
        v_new = jnp.stack([cols_ref[:, h:h + 1] for h in range(heads)])
        out = (jnp.sum(acc, axis=2, keepdims=True) + e_new * v_new) / den
        for h in range(heads):
            out_ref[:, h:h + 1] = out[h]

    def slide(h):
        def run():
            nav_ref[h] = _slide_window(cv_ref[h], cols_ref[:, h:h + 1])

        return run

    def group_b():
        gidx = lax.broadcasted_iota(jnp.int32, (GROUP_B, 1, 1), 0)
        for kv in range(N_KV_B):
            k_t, v_t = cbk_ref[kv], cbv_ref[kv]
            k_new = cols_ref[:, V_COL_KB + kv:V_COL_KB + kv + 1]
            v_new = cols_ref[:, V_COL_VB + kv:V_COL_VB + kv + 1]
            first = V_COL_QB + kv * GROUP_B
            q = jnp.stack([cols_ref[:, first + g:first + g + 1] for g in range(GROUP_B)])
            sink = jnp.zeros((GROUP_B, 1, 1), F32)
            for g in range(GROUP_B):
                sink = jnp.where(gidx == g, sink_ref[kv * GROUP_B + g], sink)
            s = jnp.sum(q * k_t[None], axis=1, keepdims=True)
            s_new = jnp.sum(q * k_new[None], axis=1, keepdims=True)
            m = jnp.maximum(jnp.maximum(jnp.max(s, axis=2, keepdims=True), s_new), sink)
            e = jnp.exp(s - m)
            e_new = jnp.exp(s_new - m)
            den = jnp.sum(e, axis=2, keepdims=True) + e_new + jnp.exp(sink - m)
            out = (jnp.sum(v_t[None] * e, axis=2, keepdims=True) + e_new * v_new[None]) / den
            for g in range(GROUP_B):
                lane = N_HEADS_A + kv * GROUP_B + g
                out_ref[:, lane:lane + 1] = out[g]
            nbk_ref[kv] = _slide_window(k_t, k_new)
            nbv_ref[kv] = _slide_window(v_t, v_new)

    return ([group_b] + [accumulate(h) for h in range(heads)] + [finalize]
            + [slide(h) for h in range(heads)])


def _columns(pieces):
    n = pieces[0].shape[0]
    return jnp.concatenate(pieces, axis=1).reshape(n, -1, HEAD_DIM).transpose(0, 2, 1)


def _ffn_host_kernel(x_ref, ga_ref, gb_ref, wg_ref, wu_ref, wd_ref, cols_ref, ck_ref,
                     h_ref, w_ref, sak_ref):
    st = {}

    def load():
        st["h"] = x_ref[...]

    def store():
        h_ref[...] = st["h"]

    ffn = _swiglu_stages(st, ga_ref, gb_ref, wg_ref, wu_ref, wd_ref)
    stages = _merge_stages(_merge_stages([load], ffn), [store])
    _run_staggered([stages], {0: _key_pass_pieces(cols_ref, ck_ref, w_ref, sak_ref)})


def _proj_kernel(x_ref, cos_ref, sin_ref, gmix_ref, win_ref,
                 qa1_ref, ka1_ref, va1_ref, qa4_ref, ka4_ref, va4_ref,
                 qa16_ref, ka16_ref, va16_ref, qb_ref, kb2_ref, vb2_ref,
                 nak_ref, nav_ref, nbk_ref, nbv_ref, stage_ref):
    tm = x_ref.shape[0]
    bm = min(ROW_BLOCK, tm)
    nblocks = tm // bm

    def emitter(j):
        rows = slice(j * bm, (j + 1) * bm)

        def decimated(z, nat_ref, dec_refs):
            nat_ref[rows, :] = z.astype(BF16)
            for c in range(QA_W // LANES):
                stage_ref[j, c] = z[:, c * LANES:(c + 1) * LANES]
            for d, ref in dec_refs:
                n = bm // d
                for r in range(d):
                    for c in range(QA_W // LANES):
                        lo = r * QA_W + c * LANES
                        piece = stage_ref[j, c, pl.ds(r, n, stride=d), :]
                        ref[j * n:(j + 1) * n, lo:lo + LANES] = piece.astype(BF16)

        def emit(qa, ka, va, qb, kb, vb):
            decimated(qa, qa1_ref, ((4, qa4_ref), (16, qa16_ref)))
            decimated(ka, ka1_ref, ((4, ka4_ref), (16, ka16_ref)))
            decimated(va, va1_ref, ((4, va4_ref), (16, va16_ref)))
            qb_ref[rows, :] = qb.astype(BF16)
            k0, k1 = _dup_heads(kb)
            kb2_ref[rows, 0:LANES] = k0.astype(BF16)
            kb2_ref[rows, LANES:2 * LANES] = k1.astype(BF16)
            v0, v1 = _dup_heads(vb)
            vb2_ref[rows, 0:LANES] = v0.astype(BF16)
            vb2_ref[rows, LANES:2 * LANES] = v1.astype(BF16)
            nak_ref[rows, :] = ka
            nav_ref[rows, :] = va
            if j == nblocks - 1:
                nbk_ref[...] = kb[bm - WIN_B:, :]
                nbv_ref[...] = vb[bm - WIN_B:, :]

        return rows, emit

    def pipeline(j):
        rows, emit = emitter(j)
        st = {}

        def load():
            st["h"] = x_ref[rows, :]

        return _merge_stages([load], _project_stages(st, rows, gmix_ref, win_ref, cos_ref, sin_ref, emit))

    _run_staggered([pipeline(j) for j in range(nblocks)])


def _stage1_sample_kernel(x_ref, cos_ref, sin_ref, g1a_ref, g1b_ref, gmix_ref,
                          wg_ref, wu_ref, wd_ref, win_ref,
                          h_ref, qa_ref, ka_ref, va_ref, qb_ref, kb_ref, vb_ref):
    def emit(*pieces):
        for ref, z in zip((qa_ref, ka_ref, va_ref, qb_ref, kb_ref, vb_ref), pieces):
            ref[...] = z

    _run_staggered([_stage1_stages({}, slice(None), x_ref, cos_ref, sin_ref, g1a_ref, g1b_ref,
                                   gmix_ref, wg_ref, wu_ref, wd_ref, win_ref, h_ref, emit)])


def _stage1_weight_specs():
    return [_whole((1, D_MODEL))] * 3 + [
        _whole((D_MODEL, D_FF)), _whole((D_MODEL, D_FF)), _whole((D_FF, D_MODEL)),
        _whole((D_MODEL, IN_W))]


def _per_sequence_spec(tiles_per_batch, *block):
    return pl.BlockSpec((None,) + block,
                        lambda b, i: (b * tiles_per_batch + i,) + (0,) * len(block))


_ARBITRARY_2D = dict(dimension_semantics=("arbitrary", "arbitrary"), vmem_limit_bytes=VMEM_LIMIT)


def _ffn_host(x, g_pre, g_post, wg, wu, wd, cols_k, cak_t):
    nb, seq, _ = x.shape
    tm = HOST_TILE
    nt = seq // tm
    n, heads, _, la = cak_t.shape
    assert nb * nt == n
    rows = pl.BlockSpec((None, tm, D_MODEL), lambda b, i: (b, i, 0))
    return pl.pallas_call(
        _ffn_host_kernel,
        out_shape=[jax.ShapeDtypeStruct(x.shape, F32),
                   jax.ShapeDtypeStruct((n, heads, la + STATS), F32),
                   jax.ShapeDtypeStruct(cak_t.shape, F32)],
        grid=(nb, nt),
        in_specs=[rows, _whole((1, D_MODEL)), _whole((1, D_MODEL)),
                  _whole((D_MODEL, D_FF)), _whole((D_MODEL, D_FF)), _whole((D_FF, D_MODEL)),
                  _per_sequence_spec(nt, HEAD_DIM, K_PASS_COLS),
                  _per_sequence_spec(nt, heads, HEAD_DIM, la)],
        out_specs=[rows, _per_sequence_spec(nt, heads, la + STATS),
                   _per_sequence_spec(nt, heads, HEAD_DIM, la)],
        compiler_params=pltpu.CompilerParams(**_ARBITRARY_2D),
        name="ffn1_host",
    )(x, g_pre, g_post, wg, wu, wd, cols_k, cak_t)


def _proj(h, cos, sin, gmix, win):
    nb, seq, _ = h.shape
    tm = ROW_TILE
    nt = seq // tm
    assert tm >= WIN_B
    first_kept = (seq - WIN_A) // tm

    def rows(width, dtype, d=1):
        shape = jax.ShapeDtypeStruct((nb, seq // d, d * width), dtype)
        spec = pl.BlockSpec((None, tm // d, d * width), lambda b, i: (b, i, 0))
        return shape, spec

    outs = [rows(QA_W, BF16)] * 3 + [rows(QA_W, BF16, 4)] * 3 + [rows(QA_W, BF16, 16)] * 3
    outs += [rows(QB_W, BF16), rows(2 * LANES, BF16), rows(2 * LANES, BF16)]
    kept = (jax.ShapeDtypeStruct((nb, WIN_A, QA_W), F32),
            pl.BlockSpec((None, tm, QA_W), lambda b, i: (b, jnp.maximum(i - first_kept, 0), 0)))
    last = (jax.ShapeDtypeStruct((nb, WIN_B, KVB_W), F32),
            pl.BlockSpec((None, WIN_B, KVB_W), lambda b, i: (b, 0, 0)))
    outs += [kept, kept, last, last]
    in_specs = [pl.BlockSpec((None, tm, D_MODEL), lambda b, i: (b, i, 0)),
                pl.BlockSpec((tm, LANES), lambda b, i: (i, 0)),
                pl.BlockSpec((tm, LANES), lambda b, i: (i, 0)),
                _whole((1, D_MODEL)), _whole((D_MODEL, IN_W))]
    return pl.pallas_call(
        _proj_kernel,
        out_shape=[o[0] for o in outs],
        grid=(nb, nt),
        in_specs=in_specs,
        out_specs=[o[1] for o in outs],
        scratch_shapes=[pltpu.VMEM((tm // ROW_BLOCK, QA_W // LANES, ROW_BLOCK, LANES), F32)],
        compiler_params=pltpu.CompilerParams(**_ARBITRARY_2D),
        name="projection",
    )(h, cos, sin, gmix, win)


def _stage1_sample(x, cos, sin, g1a, g1b, gmix, wg, wu, wd, win):
    n = x.shape[0]
    widths = (D_MODEL, QA_W, QA_W, QA_W, QB_W, KVB_W, KVB_W)
    return pl.pallas_call(
        _stage1_sample_kernel,
        out_shape=[jax.ShapeDtypeStruct((n, w), F32) for w in widths],
        grid=(1,),
        in_specs=[_whole((n, D_MODEL)), _whole((n, LANES)), _whole((n, LANES))]
        + _stage1_weight_specs(),
        out_specs=[pl.BlockSpec((n, w), lambda i: (0, 0)) for w in widths],
        compiler_params=pltpu.CompilerParams(
            dimension_semantics=("arbitrary",), vmem_limit_bytes=VMEM_LIMIT),
        name="stage1_sample",
    )(x, cos, sin, g1a, g1b, gmix, wg, wu, wd, win)


def _banded_head_pair(q, k_prev, k_cur, v_prev, v_cur, has_prev, sinks):
    rows = q.shape[0]
    nblk = rows // QBLK
    keys = 2 * QBLK

    low_lanes = lax.broadcasted_iota(jnp.int32, (rows, LANES), 1) < HEAD_DIM
    zero = jnp.zeros_like(q)
    q2 = jnp.concatenate([jnp.where(low_lanes, q, zero).reshape(nblk, QBLK, LANES),
                          jnp.where(low_lanes, zero, q).reshape(nblk, QBLK, LANES)], axis=1)
    k_all = jnp.concatenate([k_prev, k_cur], axis=0)
    v_all = jnp.concatenate([v_prev, v_cur], axis=0)
    k2 = jnp.stack([k_all[b * QBLK:b * QBLK + keys] for b in range(nblk)])
    ones = jnp.ones((keys, LANES), BF16)
    v2 = jnp.stack([jnp.concatenate([v_all[b * QBLK:b * QBLK + keys], ones], axis=1)
                    for b in range(nblk)])

    s = jnp.einsum("bqd,bkd->bqk", q2, k2, preferred_element_type=F32)
    qrow = lax.broadcasted_iota(jnp.int32, (keys, keys), 0) & (QBLK - 1)
    col = lax.broadcasted_iota(jnp.int32, (keys, keys), 1)
    in_band = jnp.logical_and(col >= qrow, col <= qrow + QBLK)
    s = jnp.where(in_band[None], s, -jnp.inf)
    first_ok = jnp.logical_or(has_prev, col >= QBLK)
    s = jnp.concatenate([jnp.where(first_ok[None], s[0:1], -jnp.inf), s[1:]], axis=0)
    m = jnp.max(s, axis=2, keepdims=True)
    if sinks is not None:
        head1 = lax.broadcasted_iota(jnp.int32, (keys, 1), 0) >= QBLK
        sink = jnp.where(head1, sinks[1], sinks[0])[None]
        m = jnp.maximum(m, sink)
    e = jnp.exp(s - m).astype(BF16)
    pv = jnp.einsum("bqk,bkd->bqd", e, v2, preferred_element_type=F32)
    acc, den = pv[:, :, :LANES], pv[:, :, LANES:]
    if sinks is not None:
        den = den + jnp.exp(sink - m)
    out2 = acc / den
    low3 = low_lanes.reshape(nblk, QBLK, LANES)
    out = jnp.where(low3, out2[:, :QBLK], out2[:, QBLK:]).reshape(rows, LANES)
    lse2 = m + jnp.log(den)
    lse = jnp.where(low3, lse2[:, :QBLK], lse2[:, QBLK:]).reshape(rows, LANES)
    return out, lse


def _banded_attention_kernel(*refs, with_sink, with_lse, shared_kv):
    refs = list(refs)
    sink_ref = refs.pop(0) if with_sink else None
    q_ref, kp_ref, kc_ref, vp_ref, vc_ref, o_ref = refs[:6]
    lse_ref = refs[6] if with_lse else None
    pairs = q_ref.shape[1] // LANES
    has_prev = pl.program_id(2) > 0
    for g in range(pairs):
        lanes = slice(g * LANES, (g + 1) * LANES)
        kg = g // shared_kv
        kv_lanes = slice(kg * LANES, (kg + 1) * LANES)
        sinks = None
        if with_sink:
            pair = pl.program_id(1) * pairs + g
            sinks = (sink_ref[2 * pair], sink_ref[2 * pair + 1])
        out, lse = _banded_head_pair(q_ref[:, lanes], kp_ref[:, kv_lanes], kc_ref[:, kv_lanes],
                                     vp_ref[:, kv_lanes], vc_ref[:, kv_lanes], has_prev, sinks)
        o_ref[:, lanes] = out.astype(o_ref.dtype)
        if with_lse:
            lse_ref[:, lanes] = lse


def _banded_attention(q, k, v, *, shared_kv=1, sinks=None, with_lse, name):
    nb, length, width = q.shape
    rows = min(ATT_ROWS, length)
    sub = rows // QBLK
    pairs = max(shared_kv, min(ATT_BLOCKS // sub, width // LANES))
    assert pairs % shared_kv == 0 and (width // LANES) % pairs == 0
    grid = (nb, width // (pairs * LANES), length // rows)
    kv_lanes = pairs * LANES // shared_kv
    q_spec = pl.BlockSpec((None, rows, pairs * LANES), lambda b, c, i: (b, i, c))
    cur_spec = pl.BlockSpec((None, rows, kv_lanes), lambda b, c, i: (b, i, c))
    prev_spec = pl.BlockSpec((None, QBLK, kv_lanes),
                             lambda b, c, i: (b, jnp.maximum(i * sub - 1, 0), c))
    in_specs = [q_spec, prev_spec, cur_spec, prev_spec, cur_spec]
    args = [q, k, k, v, v]
    if sinks is not None:
        in_specs = [pl.BlockSpec(memory_space=pltpu.SMEM)] + in_specs
        args = [sinks] + args
    out_shape = [jax.ShapeDtypeStruct(q.shape, BF16)]
    out_specs = [q_spec]
    if with_lse:
        out_shape.append(jax.ShapeDtypeStruct(q.shape, F32))
        out_specs.append(q_spec)
    return pl.pallas_call(
        functools.partial(_banded_attention_kernel, with_sink=sinks is not None, with_lse=with_lse,
                          shared_kv=shared_kv),
        out_shape=out_shape,
        grid=grid,
        in_specs=in_specs,
        out_specs=out_specs,
        compiler_params=pltpu.CompilerParams(
            dimension_semantics=("arbitrary",) * 3, vmem_limit_bytes=VMEM_LIMIT),
        name=name,
    )(*args)


def _mix_stages(st, rows, mixed, h_ref, gmix_ref, wout_ref):
    def load():
        st["oa"], st["ob"] = mixed()

    def out_proj():
        st["y"] = (_mm(st.pop("oa"), wout_ref[0:QA_W, :])
                   + _mm(st.pop("ob"), wout_ref[QA_W:QA_W + QB_W, :]))

    def mix_residual():
        st["h"] = h_ref[rows, :] + _rmsnorm(st.pop("y"), gmix_ref[...])

    return [load, out_proj, mix_residual]


def _ple_stages(st, rows, p_ref, gpa_ref, gpb_ref, wpg_ref, wpp_ref, out_ref):
    def ple_pre():
        st["u"] = _rmsnorm(st["h"], gpa_ref[...]).astype(BF16)
        st["p"] = p_ref[rows, :].astype(BF16)

    def ple_dots():
        st["gate"], st["proj"] = _mm(st.pop("u"), wpg_ref[...]), _mm(st.pop("p"), wpp_ref[...])

    def ple_residual():
        y = jax.nn.sigmoid(st.pop("gate")) * st.pop("proj")
        out_ref[rows, :] = st.pop("h") + _rmsnorm(y, gpb_ref[...])

    return [ple_pre, ple_dots, ple_residual]


def _stage3_stages(st, rows, mixed, h_ref, p_ref, norm_refs, weight_refs, out_ref):
    gmix_ref, g2a_ref, g2b_ref, gpa_ref, gpb_ref = norm_refs
    wout_ref, wg_ref, wu_ref, wd_ref, wpg_ref, wpp_ref = weight_refs
    ffn = _swiglu_stages(st, g2a_ref, g2b_ref, wg_ref, wu_ref, wd_ref)
    head = _merge_stages(_mix_stages(st, rows, mixed, h_ref, gmix_ref, wout_ref), ffn)
    return _merge_stages(head, _ple_stages(st, rows, p_ref, gpa_ref, gpb_ref, wpg_ref, wpp_ref, out_ref))


def _mix_ffn_host_kernel(sink_ref, h_ref, o1_ref, l1_ref, o4_ref, l4_ref, o16_ref, l16_ref, ob_ref,
                         gmix_ref, g2a_ref, g2b_ref, wout_ref, wg_ref, wu_ref, wd_ref,
                         cols_ref, w_ref, cv_ref, cbk_ref, cbv_ref,
                         out_ref, so_ref, sav_ref, sbk_ref, sbv_ref,
                         o4s_ref, l4s_ref, o16s_ref, l16s_ref):
    tm = h_ref.shape[0]
    bm = min(ROW_BLOCK, tm)

    def mixer(j):
        rows = slice(j * bm, (j + 1) * bm)

        def mixed():
            for d, src, dst in ((4, o4_ref, o4s_ref), (4, l4_ref, l4s_ref),
                                (16, o16_ref, o16s_ref), (16, l16_ref, l16s_ref)):
                n = bm // d
                for r in range(d):
                    for c in range(QA_W // LANES):
                        lo = r * QA_W + c * LANES
                        piece = src[j * n:(j + 1) * n, lo:lo + LANES].astype(F32)
                        dst[j, c, pl.ds(r, n, stride=d), :] = piece
            pieces = []
            for c in range(QA_W // LANES):
                cols = slice(c * LANES, (c + 1) * LANES)
                l1, l4, l16 = l1_ref[rows, cols], l4s_ref[j, c], l16s_ref[j, c]
                m = jnp.maximum(l1, jnp.maximum(l4, l16))
                w1, w4, w16 = jnp.exp(l1 - m), jnp.exp(l4 - m), jnp.exp(l16 - m)
                mix = w1 * o1_ref[rows, cols].astype(F32) + w4 * o4s_ref[j, c] + w16 * o16s_ref[j, c]
                pieces.append(mix / (w1 + w4 + w16))
            return jnp.concatenate(pieces, axis=1).astype(BF16), ob_ref[rows, :]

        return rows, mixed

    def pipeline(j):
        rows, mixed = mixer(j)
        st = {}

        def store():
            out_ref[rows, :] = st["h"]

        ffn = _swiglu_stages(st, g2a_ref, g2b_ref, wg_ref, wu_ref, wd_ref)
        head = _merge_stages(_mix_stages(st, rows, mixed, h_ref, gmix_ref, wout_ref), ffn)
        return _merge_stages(head, [store])

    pieces = _value_pass_pieces(sink_ref, cols_ref, w_ref, cv_ref, cbk_ref, cbv_ref,
                                so_ref, sav_ref, sbk_ref, sbv_ref)
    _run_staggered([pipeline(j) for j in range(tm // bm)], {0: pieces})


def _ple_kernel(h_ref, p_ref, gpa_ref, gpb_ref, wpg_ref, wpp_ref, out_ref):
    tm = h_ref.shape[0]
    bm = min(ROW_BLOCK, tm)

    def pipeline(j):
        rows = slice(j * bm, (j + 1) * bm)
        st = {}

        def load():
            st["h"] = h_ref[rows, :]

        return _merge_stages([load], _ple_stages(st, rows, p_ref, gpa_ref, gpb_ref,
                                                 wpg_ref, wpp_ref, out_ref))

    _run_staggered([pipeline(j) for j in range(tm // bm)])


def _stage3_sample_kernel(h_ref, o_ref, p_ref, *rest):
    norm_refs, weight_refs, out_ref = rest[:5], rest[5:11], rest[11]
    mixed = lambda: (o_ref[:, 0:QA_W].astype(BF16), o_ref[:, QA_W:QA_W + QB_W].astype(BF16))
    _run_staggered([_stage3_stages({}, slice(None), mixed, h_ref, p_ref, norm_refs, weight_refs,
                                   out_ref)])


def _stage3_weight_specs():
    return [_whole((1, D_MODEL))] * 5 + [
        _whole((QA_W + QB_W, D_MODEL)), _whole((D_MODEL, D_FF)), _whole((D_MODEL, D_FF)),
        _whole((D_FF, D_MODEL)), _whole((D_MODEL, D_MODEL)), _whole((D_PLE, D_MODEL))]


def _mix_ffn_host(sinks, h, o1, l1, o4, l4, o16, l16, ob, gmix, g2a, g2b, wout, wg, wu, wd,
                  cols_v, w, cav_t, cbk_t, cbv_t):
    nb, seq, _ = h.shape
    tm = HOST_TILE
    nt = seq // tm
    n, heads, _, la = cav_t.shape
    lb = cbk_t.shape[-1]
    assert nb * nt == n

    def rows(width, d=1):
        return pl.BlockSpec((None, tm // d, d * width), lambda b, i: (b, i, 0))

    in_specs = ([pl.BlockSpec(memory_space=pltpu.SMEM),
                 rows(D_MODEL), rows(QA_W), rows(QA_W), rows(QA_W, 4), rows(QA_W, 4),
                 rows(QA_W, 16), rows(QA_W, 16), rows(QB_W)]
                + [_whole((1, D_MODEL))] * 3
                + [_whole((QA_W + QB_W, D_MODEL)), _whole((D_MODEL, D_FF)), _whole((D_MODEL, D_FF)),
                   _whole((D_FF, D_MODEL))]
                + [_per_sequence_spec(nt, HEAD_DIM, V_PASS_COLS),
                   _per_sequence_spec(nt, heads, la + STATS),
                   _per_sequence_spec(nt, heads, HEAD_DIM, la),
                   _per_sequence_spec(nt, N_KV_B, HEAD_DIM, lb),
                   _per_sequence_spec(nt, N_KV_B, HEAD_DIM, lb)])
    out_shape = [jax.ShapeDtypeStruct(h.shape, F32),
                 jax.ShapeDtypeStruct((n, HEAD_DIM, OUT_COLS), F32),
                 jax.ShapeDtypeStruct(cav_t.shape, F32),
                 jax.ShapeDtypeStruct(cbk_t.shape, F32), jax.ShapeDtypeStruct(cbv_t.shape, F32)]
    out_specs = [rows(D_MODEL), _per_sequence_spec(nt, HEAD_DIM, OUT_COLS),
                 _per_sequence_spec(nt, heads, HEAD_DIM, la),
                 _per_sequence_spec(nt, N_KV_B, HEAD_DIM, lb),
                 _per_sequence_spec(nt, N_KV_B, HEAD_DIM, lb)]
    return pl.pallas_call(
        _mix_ffn_host_kernel,
        out_shape=out_shape,
        grid=(nb, nt),
        in_specs=in_specs,
        out_specs=out_specs,
        scratch_shapes=[pltpu.VMEM((1, QA_W // LANES, tm, LANES), F32)] * 4,
        compiler_params=pltpu.CompilerParams(**_ARBITRARY_2D),
        name="mix_ffn2_host",
    )(sinks, h, o1, l1, o4, l4, o16, l16, ob, gmix, g2a, g2b, wout, wg, wu, wd,
      cols_v, w, cav_t, cbk_t, cbv_t)


def _ple(h, p, gpa, gpb, wpg, wpp):
    nb, seq, _ = h.shape
    tm = ROW_TILE

    def rows(width):
        return pl.BlockSpec((None, tm, width), lambda b, i: (b, i, 0))

    return pl.pallas_call(
        _ple_kernel,
        out_shape=jax.ShapeDtypeStruct(h.shape, F32),
        grid=(nb, seq // tm),
        in_specs=[rows(D_MODEL), rows(D_PLE), _whole((1, D_MODEL)), _whole((1, D_MODEL)),
                  _whole((D_MODEL, D_MODEL)), _whole((D_PLE, D_MODEL))],
        out_specs=rows(D_MODEL),
        compiler_params=pltpu.CompilerParams(**_ARBITRARY_2D),
        name="embedding_gate",
    )(h, p, gpa, gpb, wpg, wpp)


def _stage3_sample(h, o, p, norms, weights):
    n = h.shape[0]
    return pl.pallas_call(
        _stage3_sample_kernel,
        out_shape=jax.ShapeDtypeStruct(h.shape, F32),
        grid=(1,),
        in_specs=[_whole((n, D_MODEL)), _whole((n, QA_W + QB_W)), _whole((n, D_PLE))]
        + _stage3_weight_specs(),
        out_specs=pl.BlockSpec((n, D_MODEL), lambda i: (0, 0)),
        compiler_params=pltpu.CompilerParams(
            dimension_semantics=("arbitrary",), vmem_limit_bytes=VMEM_LIMIT),
        name="stage3_sample",
    )(h, o, p, *norms, *weights)


def _rope_tables(pos):
    inv = jnp.power(ROPE_THETA, -jnp.arange(HALF, dtype=F32) * 2.0 / HEAD_DIM)
    ang = pos.astype(F32)[:, None] * inv[None, :]
    c, s = jnp.cos(ang), jnp.sin(ang)
    return jnp.concatenate([c, c, c, c], axis=-1), jnp.concatenate([-s, s, -s, s], axis=-1)


def _layer(i, hp, hs, caches, p_prompt, p_sample, norms, weights, sinks):
    (g1a, g1b, gmix_a, gmix_b, g2a, g2b, gpa, gpb) = [g[i][None, :] for g in norms]
    (wg1, wu1, wd1, win, wout, wg2, wu2, wd2, wpg, wpp) = [w[i].astype(BF16) for w in weights]
    rows_last = lambda z: z.transpose(0, 2, 3, 1)
    rows_first = lambda z: z.transpose(0, 3, 1, 2)
    cak_t, cav_t, cbk_t, cbv_t = [rows_last(c[i]) for c in caches]
    nb, seq, _ = hp.shape
    n_dec, dec_seq, _ = hs.shape
    sink = sinks[i].astype(F32)

    cos_p, sin_p = _rope_tables(jnp.arange(seq, dtype=jnp.int32))
    pos_s = jnp.broadcast_to(PAST_LEN + jnp.arange(dec_seq, dtype=jnp.int32)[None, :],
                             (n_dec, dec_seq)).reshape(-1)
    cos_s, sin_s = _rope_tables(pos_s)
    stage1_w = (g1a, g1b, gmix_a, wg1, wu1, wd1, win)
    stage3_n = (gmix_b, g2a, g2b, gpa, gpb)
    stage3_w = (wout, wg2, wu2, wd2, wpg, wpp)

    xs = hs.reshape(n_dec * dec_seq, D_MODEL)
    h1s, qa_s, ka_s, va_s, qb_s, kb_s, vb_s = _stage1_sample(xs, cos_s, sin_s, *stage1_w)
    cols_k = _columns([qa_s, ka_s])
    cols_v = _columns([va_s, qb_s, kb_s, vb_s])

    h1, w_s, nak_s = _ffn_host(hp, g1a, g1b, wg1, wu1, wd1, cols_k, cak_t)
    (qa1, ka1, va1, qa4, ka4, va4, qa16, ka16, va16, qb, kb2, vb2,
     nak_p, nav_p, nbk_p, nbv_p) = _proj(h1, cos_p, sin_p, gmix_a, win)
    o1, l1 = _banded_attention(qa1, ka1, va1, with_lse=True, name="dilated_d1")
    o4, l4 = _banded_attention(qa4, ka4, va4, with_lse=True, name="dilated_d4")
    o16, l16 = _banded_attention(qa16, ka16, va16, with_lse=True, name="dilated_d16")
    (ob,) = _banded_attention(qb, kb2, vb2, shared_kv=GROUP_B // 2, sinks=sink,
                              with_lse=False, name="swa_sink")
    h3, o_cols, nav_s, nbk_s, nbv_s = _mix_ffn_host(
        sink, h1, o1, l1, o4, l4, o16, l16, ob, gmix_b, g2a, g2b, wout, wg2, wu2, wd2,
        cols_v, w_s, cav_t, cbk_t, cbv_t)
    hp = _ple(h3, p_prompt[i], gpa, gpb, wpg, wpp)

    o_s = o_cols.transpose(0, 2, 1).reshape(n_dec, QA_W + QB_W)
    hs = _stage3_sample(h1s, o_s, p_sample[i].reshape(n_dec * dec_seq, D_PLE), stage3_n, stage3_w)
    hs = hs.reshape(n_dec, dec_seq, D_MODEL)

    heads5 = lambda z, h: z.reshape(z.shape[0], z.shape[1], h, HEAD_DIM)
    new = (heads5(nak_p, N_HEADS_A), heads5(nav_p, N_HEADS_A), heads5(nbk_p, N_KV_B),
           heads5(nbv_p, N_KV_B), rows_first(nak_s), rows_first(nav_s), rows_first(nbk_s),
           rows_first(nbv_s))
    return hp, hs, new


def kernel(x_prompt, x_sample, cache_a_k, cache_a_v, cache_b_k, cache_b_v, p_prompt, p_sample,
           norm_f1_pre, norm_f1_post, w_f1_gate, w_f1_up, w_f1_down,
           norm_mix_pre, norm_mix_post, w_in, sinks_b, w_out,
           norm_f2_pre, norm_f2_post, w_f2_gate, w_f2_up, w_f2_down,
           norm_ple_pre, norm_ple_post, w_ple_gate, w_ple_proj):
    assert x_sample.shape[1] == 1 and cache_a_k.shape[2] == WIN_A and cache_b_k.shape[2] == WIN_B
    norms = (norm_f1_pre, norm_f1_post, norm_mix_pre, norm_mix_post,
             norm_f2_pre, norm_f2_post, norm_ple_pre, norm_ple_post)
    weights = (w_f1_gate, w_f1_up, w_f1_down, w_in, w_out,
               w_f2_gate, w_f2_up, w_f2_down, w_ple_gate, w_ple_proj)
    caches = (cache_a_k, cache_a_v, cache_b_k, cache_b_v)
    hp, hs = x_prompt, x_sample
    per_layer = []
    for i in range(norm_f1_pre.shape[0]):
        hp, hs, new = _layer(i, hp, hs, caches, p_prompt, p_sample, norms, weights, sinks_b)
        per_layer.append(new)
    stacked = [jnp.stack([layer[j] for layer in per_layer]) for j in range(8)]
    return (hp, hs, *stacked)
```

```python
import functools

import jax
import jax.numpy as jnp
from jax import lax
from jax.experimental import pallas as pl
from jax.experimental.pallas import tpu as pltpu

F32 = jnp.float32
BF16 = jnp.bfloat16

D_MODEL = 1024
HEAD_DIM = 64
HALF = HEAD_DIM // 2
N_HEADS_A = 8
N_HEADS_B = 8
N_KV_B = 2
GROUP_B = N_HEADS_B // N_KV_B
DILATIONS = (1, 4, 16)
WIN_STEPS = 128
WIN_A = 2048
WIN_B = 128
PAST_LEN = 16384
D_FF = 2816
D_PLE = 256
ROPE_THETA = 10000.0
EPS = 1e-6
SCALE = HEAD_DIM ** -0.5
QA_W = N_HEADS_A * HEAD_DIM
QB_W = N_HEADS_B * HEAD_DIM
KVB_W = N_KV_B * HEAD_DIM
IN_W = 3 * QA_W + QB_W + 2 * KVB_W
LANES = 128
QBLK = 128

ROW_TILE = 128
ROW_BLOCK = 128
ATT_ROWS = 1024
ATT_BLOCKS = 32
VMEM_LIMIT = 56 * 1024 * 1024

K_PASS_COLS = 2 * N_HEADS_A
V_COL_QB, V_COL_KB, V_COL_VB = N_HEADS_A, N_HEADS_A + N_HEADS_B, N_HEADS_A + N_HEADS_B + N_KV_B
V_PASS_COLS = V_COL_VB + N_KV_B
OUT_COLS = N_HEADS_A + N_HEADS_B
STATS = LANES


def _whole(shape):
    nd = len(shape)
    return pl.BlockSpec(shape, lambda *_: (0,) * nd, pipeline_mode=pl.Buffered(1))


def _rmsnorm(x, g):
    return x * lax.rsqrt(jnp.mean(x * x, axis=-1, keepdims=True) + EPS) * g


def _mm(a, w):
    return jnp.dot(a, w, preferred_element_type=F32)


def _run_staggered(pipelines, extras=None):
    extras = extras or {}
    depth = len(pipelines[0])
    for t in range(depth + len(pipelines) - 1):
        for j, stages in enumerate(pipelines):
            if 0 <= t - j < depth:
                stages[t - j]()
        for piece in extras.get(t, ()):
            piece()


def _swiglu_stages(st, g_pre_ref, g_post_ref, wg_ref, wu_ref, wd_ref):
    def pre():
        st["u"] = _rmsnorm(st["h"], g_pre_ref[...]).astype(BF16)

    def gate_up():
        u = st.pop("u")
        st["gate"], st["up"] = _mm(u, wg_ref[...]), _mm(u, wu_ref[...])

    def activate():
        gate = st.pop("gate")
        st["act"] = (gate * jax.nn.sigmoid(gate) * st.pop("up")).astype(BF16)

    def down():
        st["y"] = _mm(st.pop("act"), wd_ref[...])

    def post():
        st["h"] = st["h"] + 0.5 * _rmsnorm(st.pop("y"), g_post_ref[...])

    return [pre, gate_up, activate, down, post]


def _merge_stages(first, second):
    last, head = first[-1], second[0]

    def both():
        last()
        head()

    return first[:-1] + [both] + second[1:]


def _rope(z, cos, sin_signed):
    rows, width = z.shape
    lane = lax.broadcasted_iota(jnp.int32, (rows, LANES), 1)
    first_half = (lane & HALF) == 0
    outs = []
    for c in range(width // LANES):
        zc = z[:, c * LANES:(c + 1) * LANES]
        partner = jnp.where(first_half,
                            pltpu.roll(zc, LANES - HALF, 1),
                            pltpu.roll(zc, HALF, 1))
        outs.append(zc * cos + partner * sin_signed)
    return outs[0] if len(outs) == 1 else jnp.concatenate(outs, axis=1)


_IN_WIDTHS = (QA_W, QA_W, QA_W, QB_W, KVB_W, KVB_W)


def _project_stages(st, rows, g_pre_ref, win_ref, cos_ref, sin_ref, emit):
    def pre():
        st["u"] = _rmsnorm(st["h"], g_pre_ref[...]).astype(BF16)

    def project():
        u, lo, pieces = st.pop("u"), 0, []
        for width in _IN_WIDTHS:
            pieces.append(_mm(u, win_ref[:, lo:lo + width]))
            lo += width
        st["z"] = pieces

    def rotate():
        qa, ka, va, qb, kb, vb = st.pop("z")
        cos, sin = cos_ref[rows, :], sin_ref[rows, :]
        emit(_rope(qa, cos, sin) * SCALE, _rope(ka, cos, sin), va,
             _rope(qb, cos, sin) * SCALE, _rope(kb, cos, sin), vb)

    return [pre, project, rotate]


def _stage1_stages(st, rows, x_ref, cos_ref, sin_ref, g1a_ref, g1b_ref, gmix_ref,
                   wg_ref, wu_ref, wd_ref, win_ref, h_ref, emit):
    def load():
        st["h"] = x_ref[rows, :]

    def store():
        h_ref[rows, :] = st["h"]

    ffn = _swiglu_stages(st, g1a_ref, g1b_ref, wg_ref, wu_ref, wd_ref)
    proj = _project_stages(st, rows, gmix_ref, win_ref, cos_ref, sin_ref, emit)
    return _merge_stages(_merge_stages([load], ffn), _merge_stages([store], proj))


def _dup_heads(x):
    lane = lax.broadcasted_iota(jnp.int32, x.shape, 1)
    swapped = pltpu.roll(x, HEAD_DIM, 1)
    lo = lane < HEAD_DIM
    return jnp.where(lo, x, swapped), jnp.where(lo, swapped, x)


def _slide_window(buf_t, new_col):
    length = buf_t.shape[1]
    lane = lax.broadcasted_iota(jnp.int32, buf_t.shape, 1)
    return jnp.where(lane == length - 1, new_col, pltpu.roll(buf_t, length - 1, 1))


def _key_pass_pieces(cols_ref, ck_ref, w_ref, nak_ref):
    heads, _, la = ck_ref.shape
    scores, new_scores = {}, {}

    def score(h):
        def run():
            q = cols_ref[:, h:h + 1]
            k_new = cols_ref[:, heads + h:heads + h + 1]
            scores[h] = jnp.sum(q * ck_ref[h], axis=0, keepdims=True)
            new_scores[h] = jnp.sum(q * k_new, axis=0, keepdims=True)

        return run

    def softmax():
        dist = la - lax.broadcasted_iota(jnp.int32, (1, la), 1)
        count = jnp.zeros((1, la), F32)
        for d in DILATIONS:
            in_pattern = jnp.logical_and((dist & (d - 1)) == 0, dist <= d * WIN_STEPS)
            count = count + in_pattern.astype(F32)
        s = jnp.concatenate([scores.pop(h) for h in range(heads)], axis=0)
        s_new = jnp.concatenate([new_scores.pop(h) for h in range(heads)], axis=0)
        s = jnp.where(count > 0.0, s, -jnp.inf)
        m = jnp.maximum(jnp.max(s, axis=1, keepdims=True), s_new)
        e = jnp.exp(s - m) * count
        e_new = float(len(DILATIONS)) * jnp.exp(s_new - m)
        den = jnp.sum(e, axis=1, keepdims=True) + e_new
        w_ref[:, 0:la] = e
        lane = lax.broadcasted_iota(jnp.int32, (heads, STATS), 1)
        w_ref[:, la:la + STATS] = jnp.where(lane < STATS // 2, e_new, den)

    def slide(h):
        def run():
            nak_ref[h] = _slide_window(ck_ref[h], cols_ref[:, heads + h:heads + h + 1])

        return run

    return [score(h) for h in range(heads)] + [softmax] + [slide(h) for h in range(heads)]


def _value_pass_pieces(sink_ref, cols_ref, w_ref, cv_ref, cbk_ref, cbv_ref,
                       out_ref, nav_ref, nbk_ref, nbv_ref):
    heads, _, la = cv_ref.shape
    partial = {}

    def accumulate(h):
        def run():
            weighted = cv_ref[h] * w_ref[h:h + 1, 0:la]
            acc = weighted[:, 0:LANES]
            for c in range(1, la // LANES):
                acc = acc + weighted[:, c * LANES:(c + 1) * LANES]
            partial[h] = acc

        return run

    def finalize():
        acc = jnp.stack([partial.pop(h) for h in range(heads)])
        e_new = jnp.stack([w_ref[h:h + 1, la:la + 1] for h in range(heads)])
        half = la + STATS // 2
        den = jnp.stack([w_ref[h:h + 1, half:half + 1] for h in range(heads)])
        v_new = jnp.stack([cols_ref[:, h:h + 1] for h in range(heads)])
        out = (jnp.sum(acc, axis=2, keepdims=True) + e_new * v_new) / den
        for h in range(heads):
            out_ref[:, h:h + 1] = out[h]

    def slide(h):
        def run():
            nav_ref[h] = _slide_window(cv_ref[h], cols_ref[:, h:h + 1])

        return run

    def group_b():
        gidx = lax.broadcasted_iota(jnp.int32, (GROUP_B, 1, 1), 0)
        for kv in range(N_KV_B):
            k_t, v_t = cbk_ref[kv], cbv_ref[kv]
            k_new = cols_ref[:, V_COL_KB + kv:V_COL_KB + kv + 1]
            v_new = cols_ref[:, V_COL_VB + kv:V_COL_VB + kv + 1]
            first = V_COL_QB + kv * GROUP_B
            q = jnp.stack([cols_ref[:, first + g:first + g + 1] for g in range(GROUP_B)])
            sink = jnp.zeros((GROUP_B, 1, 1), F32)
            for g in range(GROUP_B):
                sink = jnp.where(gidx == g, sink_ref[kv * GROUP_B + g], sink)
            s = jnp.sum(q * k_t[None], axis=1, keepdims=True)
            s_new = jnp.sum(q * k_new[None], axis=1, keepdims=True)
            m = jnp.maximum(jnp.maximum(jnp.max(s, axis=2, keepdims=True), s_new), sink)
            e = jnp.exp(s - m)
            e_new = jnp.exp(s_new - m)
            den = jnp.sum(e, axis=2, keepdims=True) + e_new + jnp.exp(sink - m)
            out = (jnp.sum(v_t[None] * e, axis=2, keepdims=True) + e_new * v_new[None]) / den
            for g in range(GROUP_B):
                lane = N_HEADS_A + kv * GROUP_B + g
                out_ref[:, lane:lane + 1] = out[g]
            nbk_ref[kv] = _slide_window(k_t, k_new)
            nbv_ref[kv] = _slide_window(v_t, v_new)

    return ([group_b] + [accumulate(h) for h in range(heads)] + [finalize]
            + [slide(h) for h in range(heads)])


def _columns(pieces):
    n = pieces[0].shape[0]
    return jnp.concatenate(pieces, axis=1).reshape(n, -1, HEAD_DIM).transpose(0, 2, 1)


def _stage1_prompt_kernel(x_ref, cos_ref, sin_ref, g1a_ref, g1b_ref, gmix_ref,
                          wg_ref, wu_ref, wd_ref, win_ref, cols_ref, ck_ref,
                          h_ref, qa1_ref, ka1_ref, va1_ref, qa4_ref, ka4_ref, va4_ref,
                          qa16_ref, ka16_ref, va16_ref, qb_ref, kb2_ref, vb2_ref,
                          nak_ref, nav_ref, nbk_ref, nbv_ref, w_ref, sak_ref, stage_ref):
    tm = x_ref.shape[0]
    bm = min(ROW_BLOCK, tm)
    nblocks = tm // bm

    def emitter(j):
        rows = slice(j * bm, (j + 1) * bm)

        def decimated(z, nat_ref, dec_refs):
            nat_ref[rows, :] = z.astype(BF16)
            for c in range(QA_W // LANES):
                stage_ref[j, c] = z[:, c * LANES:(c + 1) * LANES]
            for d, ref in dec_refs:
                n = bm // d
                for r in range(d):
                    for c in range(QA_W // LANES):
                        lo = r * QA_W + c * LANES
                        piece = stage_ref[j, c, pl.ds(r, n, stride=d), :]
                        ref[j * n:(j + 1) * n, lo:lo + LANES] = piece.astype(BF16)

        def emit(qa, ka, va, qb, kb, vb):
            decimated(qa, qa1_ref, ((4, qa4_ref), (16, qa16_ref)))
            decimated(ka, ka1_ref, ((4, ka4_ref), (16, ka16_ref)))
            decimated(va, va1_ref, ((4, va4_ref), (16, va16_ref)))
            qb_ref[rows, :] = qb.astype(BF16)
            k0, k1 = _dup_heads(kb)
            kb2_ref[rows, 0:LANES] = k0.astype(BF16)
            kb2_ref[rows, LANES:2 * LANES] = k1.astype(BF16)
            v0, v1 = _dup_heads(vb)
            vb2_ref[rows, 0:LANES] = v0.astype(BF16)
            vb2_ref[rows, LANES:2 * LANES] = v1.astype(BF16)
            nak_ref[rows, :] = ka
            nav_ref[rows, :] = va
            if j == nblocks - 1:
                nbk_ref[...] = kb[bm - WIN_B:, :]
                nbv_ref[...] = vb[bm - WIN_B:, :]

        return rows, emit

    pipelines = []
    for j in range(nblocks):
        rows, emit = emitter(j)
        pipelines.append(_stage1_stages({}, rows, x_ref, cos_ref, sin_ref, g1a_ref, g1b_ref, gmix_ref,
                                        wg_ref, wu_ref, wd_ref, win_ref, h_ref, emit))
    pieces = _key_pass_pieces(cols_ref, ck_ref, w_ref, sak_ref)
    _run_staggered(pipelines, {0: pieces})


def _stage1_sample_kernel(x_ref, cos_ref, sin_ref, g1a_ref, g1b_ref, gmix_ref,
                          wg_ref, wu_ref, wd_ref, win_ref,
                          h_ref, qa_ref, ka_ref, va_ref, qb_ref, kb_ref, vb_ref):
    def emit(*pieces):
        for ref, z in zip((qa_ref, ka_ref, va_ref, qb_ref, kb_ref, vb_ref), pieces):
            ref[...] = z

    _run_staggered([_stage1_stages({}, slice(None), x_ref, cos_ref, sin_ref, g1a_ref, g1b_ref,
                                   gmix_ref, wg_ref, wu_ref, wd_ref, win_ref, h_ref, emit)])


def _stage1_weight_specs():
    return [_whole((1, D_MODEL))] * 3 + [
        _whole((D_MODEL, D_FF)), _whole((D_MODEL, D_FF)), _whole((D_FF, D_MODEL)),
        _whole((D_MODEL, IN_W))]


def _per_sequence_spec(tiles_per_batch, *block):
    return pl.BlockSpec((None,) + block,
                        lambda b, i: (b * tiles_per_batch + i,) + (0,) * len(block))


def _stage1_prompt(x, cos, sin, g1a, g1b, gmix, wg, wu, wd, win, cols_k, cak_t):
    nb, seq, _ = x.shape
    tm = ROW_TILE
    nt = seq // tm
    n, heads, _, la = cak_t.shape
    assert nb * nt == n and tm >= WIN_B
    first_kept = (seq - WIN_A) // tm

    def rows(width, dtype, d=1):
        shape = jax.ShapeDtypeStruct((nb, seq // d, d * width), dtype)
        spec = pl.BlockSpec((None, tm // d, d * width), lambda b, i: (b, i, 0))
        return shape, spec

    outs = [rows(D_MODEL, F32)]
    outs += [rows(QA_W, BF16)] * 3 + [rows(QA_W, BF16, 4)] * 3 + [rows(QA_W, BF16, 16)] * 3
    outs += [rows(QB_W, BF16), rows(2 * LANES, BF16), rows(2 * LANES, BF16)]
    kept = (jax.ShapeDtypeStruct((nb, WIN_A, QA_W), F32),
            pl.BlockSpec((None, tm, QA_W), lambda b, i: (b, jnp.maximum(i - first_kept, 0), 0)))
    last = (jax.ShapeDtypeStruct((nb, WIN_B, KVB_W), F32),
            pl.BlockSpec((None, WIN_B, KVB_W), lambda b, i: (b, 0, 0)))
    outs += [kept, kept, last, last]
    outs += [(jax.ShapeDtypeStruct((n, heads, la + STATS), F32), _per_sequence_spec(nt, heads, la + STATS)),
             (jax.ShapeDtypeStruct(cak_t.shape, F32), _per_sequence_spec(nt, heads, HEAD_DIM, la))]
    in_specs = ([pl.BlockSpec((None, tm, D_MODEL), lambda b, i: (b, i, 0)),
                 pl.BlockSpec((tm, LANES), lambda b, i: (i, 0)),
                 pl.BlockSpec((tm, LANES), lambda b, i: (i, 0))] + _stage1_weight_specs()
                + [_per_sequence_spec(nt, HEAD_DIM, K_PASS_COLS),
                   _per_sequence_spec(nt, heads, HEAD_DIM, la)])
    return pl.pallas_call(
        _stage1_prompt_kernel,
        out_shape=[o[0] for o in outs],
        grid=(nb, nt),
        in_specs=in_specs,
        out_specs=[o[1] for o in outs],
        scratch_shapes=[pltpu.VMEM((tm // ROW_BLOCK, QA_W // LANES, ROW_BLOCK, LANES), F32)],
        compiler_params=pltpu.CompilerParams(
            dimension_semantics=("arbitrary", "arbitrary"), vmem_limit_bytes=VMEM_LIMIT),
        name="stage1_prompt",
    )(x, cos, sin, g1a, g1b, gmix, wg, wu, wd, win, cols_k, cak_t)


def _stage1_sample(x, cos, sin, g1a, g1b, gmix, wg, wu, wd, win):
    n = x.shape[0]
    widths = (D_MODEL, QA_W, QA_W, QA_W, QB_W, KVB_W, KVB_W)
    return pl.pallas_call(
        _stage1_sample_kernel,
        out_shape=[jax.ShapeDtypeStruct((n, w), F32) for w in widths],
        grid=(1,),
        in_specs=[_whole((n, D_MODEL)), _whole((n, LANES)), _whole((n, LANES))]
        + _stage1_weight_specs(),
        out_specs=[pl.BlockSpec((n, w), lambda i: (0, 0)) for w in widths],
        compiler_params=pltpu.CompilerParams(
            dimension_semantics=("arbitrary",), vmem_limit_bytes=VMEM_LIMIT),
        name="stage1_sample",
    )(x, cos, sin, g1a, g1b, gmix, wg, wu, wd, win)


def _banded_head_pair(q, k_prev, k_cur, v_prev, v_cur, has_prev, sinks):
    rows = q.shape[0]
    nblk = rows // QBLK
    keys = 2 * QBLK

    low_lanes = lax.broadcasted_iota(jnp.int32, (rows, LANES), 1) < HEAD_DIM
    zero = jnp.zeros_like(q)
    q2 = jnp.concatenate([jnp.where(low_lanes, q, zero).reshape(nblk, QBLK, LANES),
                          jnp.where(low_lanes, zero, q).reshape(nblk, QBLK, LANES)], axis=1)
    k_all = jnp.concatenate([k_prev, k_cur], axis=0)
    v_all = jnp.concatenate([v_prev, v_cur], axis=0)
    k2 = jnp.stack([k_all[b * QBLK:b * QBLK + keys] for b in range(nblk)])
    ones = jnp.ones((keys, LANES), BF16)
    v2 = jnp.stack([jnp.concatenate([v_all[b * QBLK:b * QBLK + keys], ones], axis=1)
                    for b in range(nblk)])

    s = jnp.einsum("bqd,bkd->bqk", q2, k2, preferred_element_type=F32)
    qrow = lax.broadcasted_iota(jnp.int32, (keys, keys), 0) & (QBLK - 1)
    col = lax.broadcasted_iota(jnp.int32, (keys, keys), 1)
    in_band = jnp.logical_and(col >= qrow, col <= qrow + QBLK)
    s = jnp.where(in_band[None], s, -jnp.inf)
    first_ok = jnp.logical_or(has_prev, col >= QBLK)
    s = jnp.concatenate([jnp.where(first_ok[None], s[0:1], -jnp.inf), s[1:]], axis=0)
    m = jnp.max(s, axis=2, keepdims=True)
    if sinks is not None:
        head1 = lax.broadcasted_iota(jnp.int32, (keys, 1), 0) >= QBLK
        sink = jnp.where(head1, sinks[1], sinks[0])[None]
        m = jnp.maximum(m, sink)
    e = jnp.exp(s - m).astype(BF16)
    pv = jnp.einsum("bqk,bkd->bqd", e, v2, preferred_element_type=F32)
    acc, den = pv[:, :, :LANES], pv[:, :, LANES:]
    if sinks is not None:
        den = den + jnp.exp(sink - m)
    out2 = acc / den
    low3 = low_lanes.reshape(nblk, QBLK, LANES)
    out = jnp.where(low3, out2[:, :QBLK], out2[:, QBLK:]).reshape(rows, LANES)
    lse2 = m + jnp.log(den)
    lse = jnp.where(low3, lse2[:, :QBLK], lse2[:, QBLK:]).reshape(rows, LANES)
    return out, lse


def _banded_attention_kernel(*refs, with_sink, with_lse, shared_kv):
    refs = list(refs)
    sink_ref = refs.pop(0) if with_sink else None
    q_ref, kp_ref, kc_ref, vp_ref, vc_ref, o_ref = refs[:6]
    lse_ref = refs[6] if with_lse else None
    pairs = q_ref.shape[1] // LANES
    has_prev = pl.program_id(2) > 0
    for g in range(pairs):
        lanes = slice(g * LANES, (g + 1) * LANES)
        kg = g // shared_kv
        kv_lanes = slice(kg * LANES, (kg + 1) * LANES)
        sinks = None
        if with_sink:
            pair = pl.program_id(1) * pairs + g
            sinks = (sink_ref[2 * pair], sink_ref[2 * pair + 1])
        out, lse = _banded_head_pair(q_ref[:, lanes], kp_ref[:, kv_lanes], kc_ref[:, kv_lanes],
                                     vp_ref[:, kv_lanes], vc_ref[:, kv_lanes], has_prev, sinks)
        o_ref[:, lanes] = out.astype(o_ref.dtype)
        if with_lse:
            lse_ref[:, lanes] = lse


def _banded_attention(q, k, v, *, shared_kv=1, sinks=None, with_lse, name):
    nb, length, width = q.shape
    rows = min(ATT_ROWS, length)
    sub = rows // QBLK
    pairs = max(shared_kv, min(ATT_BLOCKS // sub, width // LANES))
    assert pairs % shared_kv == 0 and (width // LANES) % pairs == 0
    grid = (nb, width // (pairs * LANES), length // rows)
    kv_lanes = pairs * LANES // shared_kv
    q_spec = pl.BlockSpec((None, rows, pairs * LANES), lambda b, c, i: (b, i, c))
    cur_spec = pl.BlockSpec((None, rows, kv_lanes), lambda b, c, i: (b, i, c))
    prev_spec = pl.BlockSpec((None, QBLK, kv_lanes),
                             lambda b, c, i: (b, jnp.maximum(i * sub - 1, 0), c))
    in_specs = [q_spec, prev_spec, cur_spec, prev_spec, cur_spec]
    args = [q, k, k, v, v]
    if sinks is not None:
        in_specs = [pl.BlockSpec(memory_space=pltpu.SMEM)] + in_specs
        args = [sinks] + args
    out_shape = [jax.ShapeDtypeStruct(q.shape, BF16)]
    out_specs = [q_spec]
    if with_lse:
        out_shape.append(jax.ShapeDtypeStruct(q.shape, F32))
        out_specs.append(q_spec)
    return pl.pallas_call(
        functools.partial(_banded_attention_kernel, with_sink=sinks is not None, with_lse=with_lse,
                          shared_kv=shared_kv),
        out_shape=out_shape,
        grid=grid,
        in_specs=in_specs,
        out_specs=out_specs,
        compiler_params=pltpu.CompilerParams(
            dimension_semantics=("arbitrary",) * 3, vmem_limit_bytes=VMEM_LIMIT),
        name=name,
    )(*args)


def _stage3_stages(st, rows, mixed, h_ref, p_ref, norm_refs, weight_refs, out_ref):
    gmix_ref, g2a_ref, g2b_ref, gpa_ref, gpb_ref = norm_refs
    wout_ref, wg_ref, wu_ref, wd_ref, wpg_ref, wpp_ref = weight_refs

    def load():
        st["oa"], st["ob"] = mixed()

    def out_proj():
        st["y"] = (_mm(st.pop("oa"), wout_ref[0:QA_W, :])
                   + _mm(st.pop("ob"), wout_ref[QA_W:QA_W + QB_W, :]))

    def mix_residual():
        st["h"] = h_ref[rows, :] + _rmsnorm(st.pop("y"), gmix_ref[...])

    def ple_pre():
        st["u"] = _rmsnorm(st["h"], gpa_ref[...]).astype(BF16)
        st["p"] = p_ref[rows, :].astype(BF16)

    def ple_dots():
        st["gate"], st["proj"] = _mm(st.pop("u"), wpg_ref[...]), _mm(st.pop("p"), wpp_ref[...])

    def ple_residual():
        y = jax.nn.sigmoid(st.pop("gate")) * st.pop("proj")
        out_ref[rows, :] = st.pop("h") + _rmsnorm(y, gpb_ref[...])

    ffn = _swiglu_stages(st, g2a_ref, g2b_ref, wg_ref, wu_ref, wd_ref)
    head = _merge_stages([load, out_proj, mix_residual], ffn)
    return _merge_stages(head, [ple_pre, ple_dots, ple_residual])


def _stage3_prompt_kernel(sink_ref, h_ref, o1_ref, l1_ref, o4_ref, l4_ref, o16_ref, l16_ref,
                          ob_ref, p_ref, *rest):
    norm_refs, weight_refs = rest[:5], rest[5:11]
    cols_ref, w_ref, cv_ref, cbk_ref, cbv_ref = rest[11:16]
    out_ref, so_ref, sav_ref, sbk_ref, sbv_ref = rest[16:21]
    o4s_ref, l4s_ref, o16s_ref, l16s_ref = rest[21:]
    tm = h_ref.shape[0]
    bm = min(ROW_BLOCK, tm)

    def mixer(j):
        rows = slice(j * bm, (j + 1) * bm)

        def mixed():
            for d, src, dst in ((4, o4_ref, o4s_ref), (4, l4_ref, l4s_ref),
                                (16, o16_ref, o16s_ref), (16, l16_ref, l16s_ref)):
                n = bm // d
                for r in range(d):
                    for c in range(QA_W // LANES):
                        lo = r * QA_W + c * LANES
                        piece = src[j * n:(j + 1) * n, lo:lo + LANES].astype(F32)
                        dst[j, c, pl.ds(r, n, stride=d), :] = piece
            pieces = []
            for c in range(QA_W // LANES):
                cols = slice(c * LANES, (c + 1) * LANES)
                l1, l4, l16 = l1_ref[rows, cols], l4s_ref[j, c], l16s_ref[j, c]
                m = jnp.maximum(l1, jnp.maximum(l4, l16))
                w1, w4, w16 = jnp.exp(l1 - m), jnp.exp(l4 - m), jnp.exp(l16 - m)
                mix = w1 * o1_ref[rows, cols].astype(F32) + w4 * o4s_ref[j, c] + w16 * o16s_ref[j, c]
                pieces.append(mix / (w1 + w4 + w16))
            return jnp.concatenate(pieces, axis=1).astype(BF16), ob_ref[rows, :]

        return rows, mixed

    pipelines = []
    for j in range(tm // bm):
        rows, mixed = mixer(j)
        pipelines.append(_stage3_stages({}, rows, mixed, h_ref, p_ref, norm_refs, weight_refs, out_ref))
    pieces = _value_pass_pieces(sink_ref, cols_ref, w_ref, cv_ref, cbk_ref, cbv_ref,
                                so_ref, sav_ref, sbk_ref, sbv_ref)
    _run_staggered(pipelines, {0: pieces})


def _stage3_sample_kernel(h_ref, o_ref, p_ref, *rest):
    norm_refs, weight_refs, out_ref = rest[:5], rest[5:11], rest[11]
    mixed = lambda: (o_ref[:, 0:QA_W].astype(BF16), o_ref[:, QA_W:QA_W + QB_W].astype(BF16))
    _run_staggered([_stage3_stages({}, slice(None), mixed, h_ref, p_ref, norm_refs, weight_refs,
                                   out_ref)])


def _stage3_weight_specs():
    return [_whole((1, D_MODEL))] * 5 + [
        _whole((QA_W + QB_W, D_MODEL)), _whole((D_MODEL, D_FF)), _whole((D_MODEL, D_FF)),
        _whole((D_FF, D_MODEL)), _whole((D_MODEL, D_MODEL)), _whole((D_PLE, D_MODEL))]


def _stage3_prompt(sinks, h, o1, l1, o4, l4, o16, l16, ob, p, norms, weights,
                   cols_v, w, cav_t, cbk_t, cbv_t):
    nb, seq, _ = h.shape
    tm = ROW_TILE
    nt = seq // tm
    n, heads, _, la = cav_t.shape
    lb = cbk_t.shape[-1]
    assert nb * nt == n

    def rows(width, d=1):
        return pl.BlockSpec((None, tm // d, d * width), lambda b, i: (b, i, 0))

    in_specs = ([pl.BlockSpec(memory_space=pltpu.SMEM),
                 rows(D_MODEL), rows(QA_W), rows(QA_W), rows(QA_W, 4), rows(QA_W, 4),
                 rows(QA_W, 16), rows(QA_W, 16), rows(QB_W), rows(D_PLE)] + _stage3_weight_specs()
                + [_per_sequence_spec(nt, HEAD_DIM, V_PASS_COLS),
                   _per_sequence_spec(nt, heads, la + STATS),
                   _per_sequence_spec(nt, heads, HEAD_DIM, la),
                   _per_sequence_spec(nt, N_KV_B, HEAD_DIM, lb),
                   _per_sequence_spec(nt, N_KV_B, HEAD_DIM, lb)])
    out_shape = [jax.ShapeDtypeStruct(h.shape, F32),
                 jax.ShapeDtypeStruct((n, HEAD_DIM, OUT_COLS), F32),
                 jax.ShapeDtypeStruct(cav_t.shape, F32),
                 jax.ShapeDtypeStruct(cbk_t.shape, F32), jax.ShapeDtypeStruct(cbv_t.shape, F32)]
    out_specs = [rows(D_MODEL), _per_sequence_spec(nt, HEAD_DIM, OUT_COLS),
                 _per_sequence_spec(nt, heads, HEAD_DIM, la),
                 _per_sequence_spec(nt, N_KV_B, HEAD_DIM, lb),
                 _per_sequence_spec(nt, N_KV_B, HEAD_DIM, lb)]
    return pl.pallas_call(
        _stage3_prompt_kernel,
        out_shape=out_shape,
        grid=(nb, nt),
        in_specs=in_specs,
        out_specs=out_specs,
        scratch_shapes=[pltpu.VMEM((tm // ROW_BLOCK, QA_W // LANES, ROW_BLOCK, LANES), F32)] * 4,
        compiler_params=pltpu.CompilerParams(
            dimension_semantics=("arbitrary", "arbitrary"), vmem_limit_bytes=VMEM_LIMIT),
        name="stage3_prompt",
    )(sinks, h, o1, l1, o4, l4, o16, l16, ob, p, *norms, *weights, cols_v, w, cav_t, cbk_t, cbv_t)


def _stage3_sample(h, o, p, norms, weights):
    n = h.shape[0]
    return pl.pallas_call(
        _stage3_sample_kernel,
        out_shape=jax.ShapeDtypeStruct(h.shape, F32),
        grid=(1,),
        in_specs=[_whole((n, D_MODEL)), _whole((n, QA_W + QB_W)), _whole((n, D_PLE))]
        + _stage3_weight_specs(),
        out_specs=pl.BlockSpec((n, D_MODEL), lambda i: (0, 0)),
        compiler_params=pltpu.CompilerParams(
            dimension_semantics=("arbitrary",), vmem_limit_bytes=VMEM_LIMIT),
        name="stage3_sample",
    )(h, o, p, *norms, *weights)


def _rope_tables(pos):
    inv = jnp.power(ROPE_THETA, -jnp.arange(HALF, dtype=F32) * 2.0 / HEAD_DIM)
    ang = pos.astype(F32)[:, None] * inv[None, :]
    c, s = jnp.cos(ang), jnp.sin(ang)
    return jnp.concatenate([c, c, c, c], axis=-1), jnp.concatenate([-s, s, -s, s], axis=-1)


def _layer(i, hp, hs, caches, p_prompt, p_sample, norms, weights, sinks):
    (g1a, g1b, gmix_a, gmix_b, g2a, g2b, gpa, gpb) = [g[i][None, :] for g in norms]
    (wg1, wu1, wd1, win, wout, wg2, wu2, wd2, wpg, wpp) = [w[i].astype(BF16) for w in weights]
    rows_last = lambda z: z.transpose(0, 2, 3, 1)
    rows_first = lambda z: z.transpose(0, 3, 1, 2)
    cak_t, cav_t, cbk_t, cbv_t = [rows_last(c[i]) for c in caches]
    nb, seq, _ = hp.shape
    n_dec, dec_seq, _ = hs.shape
    sink = sinks[i].astype(F32)

    cos_p, sin_p = _rope_tables(jnp.arange(seq, dtype=jnp.int32))
    pos_s = jnp.broadcast_to(PAST_LEN + jnp.arange(dec_seq, dtype=jnp.int32)[None, :],
                             (n_dec, dec_seq)).reshape(-1)
    cos_s, sin_s = _rope_tables(pos_s)
    stage1_w = (g1a, g1b, gmix_a, wg1, wu1, wd1, win)
    stage3_n = (gmix_b, g2a, g2b, gpa, gpb)
    stage3_w = (wout, wg2, wu2, wd2, wpg, wpp)

    xs = hs.reshape(n_dec * dec_seq, D_MODEL)
    h1s, qa_s, ka_s, va_s, qb_s, kb_s, vb_s = _stage1_sample(xs, cos_s, sin_s, *stage1_w)
    cols_k = _columns([qa_s, ka_s])
    cols_v = _columns([va_s, qb_s, kb_s, vb_s])

    (h1, qa1, ka1, va1, qa4, ka4, va4, qa16, ka16, va16, qb, kb2, vb2,
     nak_p, nav_p, nbk_p, nbv_p, w_s, nak_s) = _stage1_prompt(hp, cos_p, sin_p, *stage1_w,
                                                             cols_k, cak_t)
    o1, l1 = _banded_attention(qa1, ka1, va1, with_lse=True, name="dilated_d1")
    o4, l4 = _banded_attention(qa4, ka4, va4, with_lse=True, name="dilated_d4")
    o16, l16 = _banded_attention(qa16, ka16, va16, with_lse=True, name="dilated_d16")
    (ob,) = _banded_attention(qb, kb2, vb2, shared_kv=GROUP_B // 2, sinks=sink,
                              with_lse=False, name="swa_sink")
    hp, o_cols, nav_s, nbk_s, nbv_s = _stage3_prompt(
        sink, h1, o1, l1, o4, l4, o16, l16, ob, p_prompt[i], stage3_n, stage3_w,
        cols_v, w_s, cav_t, cbk_t, cbv_t)

    o_s = o_cols.transpose(0, 2, 1).reshape(n_dec, QA_W + QB_W)
    hs = _stage3_sample(h1s, o_s, p_sample[i].reshape(n_dec * dec_seq, D_PLE), stage3_n, stage3_w)
    hs = hs.reshape(n_dec, dec_seq, D_MODEL)

    heads5 = lambda z, h: z.reshape(z.shape[0], z.shape[1], h, HEAD_DIM)
    new = (heads5(nak_p, N_HEADS_A), heads5(nav_p, N_HEADS_A), heads5(nbk_p, N_KV_B),
           heads5(nbv_p, N_KV_B), rows_first(nak_s), rows_first(nav_s), rows_first(nbk_s),
           rows_first(nbv_s))
    return hp, hs, new


def kernel(x_prompt, x_sample, cache_a_k, cache_a_v, cache_b_k, cache_b_v, p_prompt, p_sample,
           norm_f1_pre, norm_f1_post, w_f1_gate, w_f1_up, w_f1_down,
           norm_mix_pre, norm_mix_post, w_in, sinks_b, w_out,
           norm_f2_pre, norm_f2_post, w_f2_gate, w_f2_up, w_f2_down,
           norm_ple_pre, norm_ple_post, w_ple_gate, w_ple_proj):
    assert x_sample.shape[1] == 1 and cache_a_k.shape[2] == WIN_A and cache_b_k.shape[2] == WIN_B
    norms = (norm_f1_pre, norm_f1_post, norm_mix_pre, norm_mix_post,
             norm_f2_pre, norm_f2_post, norm_ple_pre, norm_ple_post)
    weights = (w_f1_gate, w_f1_up, w_f1_down, w_in, w_out,
               w_f2_gate, w_f2_up, w_f2_down, w_ple_gate, w_ple_proj)
    caches = (cache_a_k, cache_a_v, cache_b_k, cache_b_v)
    hp, hs = x_prompt, x_sample
    per_layer = []
    for i in range(norm_f1_pre.shape[0]):
        hp, hs, new = _layer(i, hp, hs, caches, p_prompt, p_sample, norms, weights, sinks_b)
        per_layer.append(new)
    stacked = [jnp.stack([layer[j] for layer in per_layer]) for j in range(8)]
    return (hp, hs, *stacked)
```

```python
import functools

import jax
import jax.numpy as jnp
from jax import lax
from jax.experimental import pallas as pl
from jax.experimental.pallas import tpu as pltpu

F32 = jnp.float32
BF16 = jnp.bfloat16

D_MODEL = 1024
HEAD_DIM = 64
HALF = HEAD_DIM // 2
N_HEADS_A = 8
N_HEADS_B = 8
N_KV_B = 2
GROUP_B = N_HEADS_B // N_KV_B
DILATIONS = (1, 4, 16)
WIN_STEPS = 128
WIN_A = 2048
WIN_B = 128
PAST_LEN = 16384
D_FF = 2816
D_PLE = 256
ROPE_THETA = 10000.0
EPS = 1e-6
SCALE = HEAD_DIM ** -0.5
QA_W = N_HEADS_A * HEAD_DIM
QB_W = N_HEADS_B * HEAD_DIM
KVB_W = N_KV_B * HEAD_DIM
IN_W = 3 * QA_W + QB_W + 2 * KVB_W
LANES = 128
QBLK = 128

ROW_TILE = 128
ROW_BLOCK = 128
ATT_ROWS = 1024
ATT_BLOCKS = 32
VMEM_LIMIT = 56 * 1024 * 1024

K_PASS_COLS = 2 * N_HEADS_A
V_COL_QB, V_COL_KB, V_COL_VB = N_HEADS_A, N_HEADS_A + N_HEADS_B, N_HEADS_A + N_HEADS_B + N_KV_B
V_PASS_COLS = V_COL_VB + N_KV_B
OUT_COLS = N_HEADS_A + N_HEADS_B
STATS = LANES


def _whole(shape):
    nd = len(shape)
    return pl.BlockSpec(shape, lambda *_: (0,) * nd, pipeline_mode=pl.Buffered(1))


def _rmsnorm(x, g):
    return x * lax.rsqrt(jnp.mean(x * x, axis=-1, keepdims=True) + EPS) * g


def _mm(a, w):
    return jnp.dot(a, w, preferred_element_type=F32)


def _run_staggered(pipelines, extras=None):
    extras = extras or {}
    depth = len(pipelines[0])
    for t in range(depth + len(pipelines) - 1):
        for j, stages in enumerate(pipelines):
            if 0 <= t - j < depth:
                stages[t - j]()
        for piece in extras.get(t, ()):
            piece()


def _swiglu_stages(st, g_pre_ref, g_post_ref, wg_ref, wu_ref, wd_ref):
    def pre():
        st["u"] = _rmsnorm(st["h"], g_pre_ref[...]).astype(BF16)

    def gate_up():
        u = st.pop("u")
        st["gate"], st["up"] = _mm(u, wg_ref[...]), _mm(u, wu_ref[...])

    def activate():
        gate = st.pop("gate")
        st["act"] = (gate * jax.nn.sigmoid(gate) * st.pop("up")).astype(BF16)

    def down():
        st["y"] = _mm(st.pop("act"), wd_ref[...])

    def post():
        st["h"] = st["h"] + 0.5 * _rmsnorm(st.pop("y"), g_post_ref[...])

    return [pre, gate_up, activate, down, post]


def _merge_stages(first, second):
    last, head = first[-1], second[0]

    def both():
        last()
        head()

    return first[:-1] + [both] + second[1:]


def _rope(z, cos, sin_signed):
    rows, width = z.shape
    lane = lax.broadcasted_iota(jnp.int32, (rows, LANES), 1)
    first_half = (lane & HALF) == 0
    outs = []
    for c in range(width // LANES):
        zc = z[:, c * LANES:(c + 1) * LANES]
        partner = jnp.where(first_half,
                            pltpu.roll(zc, LANES - HALF, 1),
                            pltpu.roll(zc, HALF, 1))
        outs.append(zc * cos + partner * sin_signed)
    return outs[0] if len(outs) == 1 else jnp.concatenate(outs, axis=1)


_IN_WIDTHS = (QA_W, QA_W, QA_W, QB_W, KVB_W, KVB_W)


def _project_stages(st, rows, g_pre_ref, win_ref, cos_ref, sin_ref, emit):
    def pre():
        st["u"] = _rmsnorm(st["h"], g_pre_ref[...]).astype(BF16)

    def project():
        u, lo, pieces = st.pop("u"), 0, []
        for width in _IN_WIDTHS:
            pieces.append(_mm(u, win_ref[:, lo:lo + width]))
            lo += width
        st["z"] = pieces

    def rotate():
        qa, ka, va, qb, kb, vb = st.pop("z")
        cos, sin = cos_ref[rows, :], sin_ref[rows, :]
        emit(_rope(qa, cos, sin) * SCALE, _rope(ka, cos, sin), va,
             _rope(qb, cos, sin) * SCALE, _rope(kb, cos, sin), vb)

    return [pre, project, rotate]


def _stage1_stages(st, rows, x_ref, cos_ref, sin_ref, g1a_ref, g1b_ref, gmix_ref,
                   wg_ref, wu_ref, wd_ref, win_ref, h_ref, emit):
    def load():
        st["h"] = x_ref[rows, :]

    def store():
        h_ref[rows, :] = st["h"]

    ffn = _swiglu_stages(st, g1a_ref, g1b_ref, wg_ref, wu_ref, wd_ref)
    proj = _project_stages(st, rows, gmix_ref, win_ref, cos_ref, sin_ref, emit)
    return _merge_stages(_merge_stages([load], ffn), _merge_stages([store], proj))


def _dup_heads(x):
    lane = lax.broadcasted_iota(jnp.int32, x.shape, 1)
    swapped = pltpu.roll(x, HEAD_DIM, 1)
    lo = lane < HEAD_DIM
    return jnp.where(lo, x, swapped), jnp.where(lo, swapped, x)


def _slide_window(buf_t, new_col):
    length = buf_t.shape[1]
    lane = lax.broadcasted_iota(jnp.int32, buf_t.shape, 1)
    return jnp.where(lane == length - 1, new_col, pltpu.roll(buf_t, length - 1, 1))


def _key_pass_pieces(cols_ref, ck_ref, w_ref, nak_ref):
    heads, _, la = ck_ref.shape
    scores, new_scores = {}, {}

    def score(h):
        def run():
            q = cols_ref[:, h:h + 1]
            k_new = cols_ref[:, heads + h:heads + h + 1]
            scores[h] = jnp.sum(q * ck_ref[h], axis=0, keepdims=True)
            new_scores[h] = jnp.sum(q * k_new, axis=0, keepdims=True)

        return run

    def softmax():
        dist = la - lax.broadcasted_iota(jnp.int32, (1, la), 1)
        count = jnp.zeros((1, la), F32)
        for d in DILATIONS:
            in_pattern = jnp.logical_and((dist & (d - 1)) == 0, dist <= d * WIN_STEPS)
            count = count + in_pattern.astype(F32)
        s = jnp.concatenate([scores.pop(h) for h in range(heads)], axis=0)
        s_new = jnp.concatenate([new_scores.pop(h) for h in range(heads)], axis=0)
        s = jnp.where(count > 0.0, s, -jnp.inf)
        m = jnp.maximum(jnp.max(s, axis=1, keepdims=True), s_new)
        e = jnp.exp(s - m) * count
        e_new = float(len(DILATIONS)) * jnp.exp(s_new - m)
        den = jnp.sum(e, axis=1, keepdims=True) + e_new
        w_ref[:, 0:la] = e
        lane = lax.broadcasted_iota(jnp.int32, (heads, STATS), 1)
        w_ref[:, la:la + STATS] = jnp.where(lane < STATS // 2, e_new, den)

    def slide(h):
        def run():
            nak_ref[h] = _slide_window(ck_ref[h], cols_ref[:, heads + h:heads + h + 1])

        return run

    return [score(h) for h in range(heads)] + [softmax] + [slide(h) for h in range(heads)]


def _value_pass_pieces(sink_ref, cols_ref, w_ref, cv_ref, cbk_ref, cbv_ref,
                       out_ref, nav_ref, nbk_ref, nbv_ref):
    heads, _, la = cv_ref.shape
    partial = {}

    def accumulate(h):
        def run():
            weighted = cv_ref[h] * w_ref[h:h + 1, 0:la]
            acc = weighted[:, 0:LANES]
            for c in range(1, la // LANES):
                acc = acc + weighted[:, c * LANES:(c + 1) * LANES]
            partial[h] = acc

        return run

    def finalize():
        acc = jnp.stack([partial.pop(h) for h in range(heads)])
        e_new = jnp.stack([w_ref[h:h + 1, la:la + 1] for h in range(heads)])
        half = la + STATS // 2
        den = jnp.stack([w_ref[h:h + 1, half:half + 1] for h in range(heads)])
        v_new = jnp.stack([cols_ref[:, h:h + 1] for h in range(heads)])
        out = (jnp.sum(acc, axis=2, keepdims=True) + e_new * v_new) / den
        for h in range(heads):
            out_ref[:, h:h + 1] = out[h]

    def slide(h):
        def run():
            nav_ref[h] = _slide_window(cv_ref[h], cols_ref[:, h:h + 1])

        return run

    def group_b():
        gidx = lax.broadcasted_iota(jnp.int32, (GROUP_B, 1, 1), 0)
        for kv in range(N_KV_B):
            k_t, v_t = cbk_ref[kv], cbv_ref[kv]
            k_new = cols_ref[:, V_COL_KB + kv:V_COL_KB + kv + 1]
            v_new = cols_ref[:, V_COL_VB + kv:V_COL_VB + kv + 1]
            first = V_COL_QB + kv * GROUP_B
            q = jnp.stack([cols_ref[:, first + g:first + g + 1] for g in range(GROUP_B)])
            sink = jnp.zeros((GROUP_B, 1, 1), F32)
            for g in range(GROUP_B):
                sink = jnp.where(gidx == g, sink_ref[kv * GROUP_B + g], sink)
            s = jnp.sum(q * k_t[None], axis=1, keepdims=True)
            s_new = jnp.sum(q * k_new[None], axis=1, keepdims=True)
            m = jnp.maximum(jnp.maximum(jnp.max(s, axis=2, keepdims=True), s_new), sink)
            e = jnp.exp(s - m)
            e_new = jnp.exp(s_new - m)
            den = jnp.sum(e, axis=2, keepdims=True) + e_new + jnp.exp(sink - m)
            out = (jnp.sum(v_t[None] * e, axis=2, keepdims=True) + e_new * v_new[None]) / den
            for g in range(GROUP_B):
                lane = N_HEADS_A + kv * GROUP_B + g
                out_ref[:, lane:lane + 1] = out[g]
            nbk_ref[kv] = _slide_window(k_t, k_new)
            nbv_ref[kv] = _slide_window(v_t, v_new)

    return ([group_b] + [accumulate(h) for h in range(heads)] + [finalize]
            + [slide(h) for h in range(heads)])


def _columns(pieces):
    n = pieces[0].shape[0]
    return jnp.concatenate(pieces, axis=1).reshape(n, -1, HEAD_DIM).transpose(0, 2, 1)


def _stage1_prompt_kernel(ck_ref, cols_ref, x_ref, cos_ref, sin_ref, g1a_ref, g1b_ref, gmix_ref,
                          wg_ref, wu_ref, wd_ref, win_ref,
                          sak_ref, w_ref, h_ref, qa1_ref, ka1_ref, va1_ref, qa4_ref, ka4_ref, va4_ref,
                          qa16_ref, ka16_ref, va16_ref, qb_ref, kb2_ref, vb2_ref,
                          nak_ref, nav_ref, nbk_ref, nbv_ref, stage_ref):
    tm = x_ref.shape[0]
    bm = min(ROW_BLOCK, tm)
    nblocks = tm // bm

    def emitter(j):
        rows = slice(j * bm, (j + 1) * bm)

        def decimated(z, nat_ref, dec_refs):
            nat_ref[rows, :] = z.astype(BF16)
            for c in range(QA_W // LANES):
                stage_ref[j, c] = z[:, c * LANES:(c + 1) * LANES]
            for d, ref in dec_refs:
                n = bm // d
                for r in range(d):
                    for c in range(QA_W // LANES):
                        lo = r * QA_W + c * LANES
                        piece = stage_ref[j, c, pl.ds(r, n, stride=d), :]
                        ref[j * n:(j + 1) * n, lo:lo + LANES] = piece.astype(BF16)

        def emit(qa, ka, va, qb, kb, vb):
            decimated(qa, qa1_ref, ((4, qa4_ref), (16, qa16_ref)))
            decimated(ka, ka1_ref, ((4, ka4_ref), (16, ka16_ref)))
            decimated(va, va1_ref, ((4, va4_ref), (16, va16_ref)))
            qb_ref[rows, :] = qb.astype(BF16)
            k0, k1 = _dup_heads(kb)
            kb2_ref[rows, 0:LANES] = k0.astype(BF16)
            kb2_ref[rows, LANES:2 * LANES] = k1.astype(BF16)
            v0, v1 = _dup_heads(vb)
            vb2_ref[rows, 0:LANES] = v0.astype(BF16)
            vb2_ref[rows, LANES:2 * LANES] = v1.astype(BF16)
            nak_ref[rows, :] = ka
            nav_ref[rows, :] = va
            if j == nblocks - 1:
                nbk_ref[...] = kb[bm - WIN_B:, :]
                nbv_ref[...] = vb[bm - WIN_B:, :]

        return rows, emit

    pipelines = []
    for j in range(nblocks):
        rows, emit = emitter(j)
        pipelines.append(_stage1_stages({}, rows, x_ref, cos_ref, sin_ref, g1a_ref, g1b_ref, gmix_ref,
                                        wg_ref, wu_ref, wd_ref, win_ref, h_ref, emit))
    pieces = _key_pass_pieces(cols_ref, ck_ref, w_ref, sak_ref)
    _run_staggered(pipelines, {0: pieces})


def _stage1_sample_kernel(x_ref, cos_ref, sin_ref, g1a_ref, g1b_ref, gmix_ref,
                          wg_ref, wu_ref, wd_ref, win_ref,
                          h_ref, qa_ref, ka_ref, va_ref, qb_ref, kb_ref, vb_ref):
    def emit(*pieces):
        for ref, z in zip((qa_ref, ka_ref, va_ref, qb_ref, kb_ref, vb_ref), pieces):
            ref[...] = z

    _run_staggered([_stage1_stages({}, slice(None), x_ref, cos_ref, sin_ref, g1a_ref, g1b_ref,
                                   gmix_ref, wg_ref, wu_ref, wd_ref, win_ref, h_ref, emit)])


def _stage1_weight_specs():
    return [_whole((1, D_MODEL))] * 3 + [
        _whole((D_MODEL, D_FF)), _whole((D_MODEL, D_FF)), _whole((D_FF, D_MODEL)),
        _whole((D_MODEL, IN_W))]


def _per_sequence_spec(tiles_per_batch, *block):
    return pl.BlockSpec((None,) + block,
                        lambda b, i: (b * tiles_per_batch + i,) + (0,) * len(block))


def _stage1_prompt(x, cos, sin, g1a, g1b, gmix, wg, wu, wd, win, cols_k, cak_t):
    nb, seq, _ = x.shape
    tm = ROW_TILE
    nt = seq // tm
    n, heads, _, la = cak_t.shape
    assert nb * nt == n and tm >= WIN_B
    first_kept = (seq - WIN_A) // tm

    def rows(width, dtype, d=1):
        shape = jax.ShapeDtypeStruct((nb, seq // d, d * width), dtype)
        spec = pl.BlockSpec((None, tm // d, d * width), lambda b, i: (b, i, 0))
        return shape, spec

    outs = [(jax.ShapeDtypeStruct(cak_t.shape, F32), _per_sequence_spec(nt, heads, HEAD_DIM, la)),
            (jax.ShapeDtypeStruct((n, heads, la + STATS), F32), _per_sequence_spec(nt, heads, la + STATS))]
    outs += [rows(D_MODEL, F32)]
    outs += [rows(QA_W, BF16)] * 3 + [rows(QA_W, BF16, 4)] * 3 + [rows(QA_W, BF16, 16)] * 3
    outs += [rows(QB_W, BF16), rows(2 * LANES, BF16), rows(2 * LANES, BF16)]
    kept = (jax.ShapeDtypeStruct((nb, WIN_A, QA_W), F32),
            pl.BlockSpec((None, tm, QA_W), lambda b, i: (b, jnp.maximum(i - first_kept, 0), 0)))
    last = (jax.ShapeDtypeStruct((nb, WIN_B, KVB_W), F32),
            pl.BlockSpec((None, WIN_B, KVB_W), lambda b, i: (b, 0, 0)))
    outs += [kept, kept, last, last]
    in_specs = ([_per_sequence_spec(nt, heads, HEAD_DIM, la),
                 _per_sequence_spec(nt, HEAD_DIM, K_PASS_COLS),
                 pl.BlockSpec((None, tm, D_MODEL), lambda b, i: (b, i, 0)),
                 pl.BlockSpec((tm, LANES), lambda b, i: (i, 0)),
                 pl.BlockSpec((tm, LANES), lambda b, i: (i, 0))] + _stage1_weight_specs())
    return pl.pallas_call(
        _stage1_prompt_kernel,
        out_shape=[o[0] for o in outs],
        grid=(nb, nt),
        in_specs=in_specs,
        out_specs=[o[1] for o in outs],
        scratch_shapes=[pltpu.VMEM((tm // ROW_BLOCK, QA_W // LANES, ROW_BLOCK, LANES), F32)],
        compiler_params=pltpu.CompilerParams(
            dimension_semantics=("arbitrary", "arbitrary"), vmem_limit_bytes=VMEM_LIMIT),
        name="stage1_prompt",
    )(cak_t, cols_k, x, cos, sin, g1a, g1b, gmix, wg, wu, wd, win)


def _stage1_sample(x, cos, sin, g1a, g1b, gmix, wg, wu, wd, win):
    n = x.shape[0]
    widths = (D_MODEL, QA_W, QA_W, QA_W, QB_W, KVB_W, KVB_W)
    return pl.pallas_call(
        _stage1_sample_kernel,
        out_shape=[jax.ShapeDtypeStruct((n, w), F32) for w in widths],
        grid=(1,),
        in_specs=[_whole((n, D_MODEL)), _whole((n, LANES)), _whole((n, LANES))]
        + _stage1_weight_specs(),
        out_specs=[pl.BlockSpec((n, w), lambda i: (0, 0)) for w in widths],
        compiler_params=pltpu.CompilerParams(
            dimension_semantics=("arbitrary",), vmem_limit_bytes=VMEM_LIMIT),
        name="stage1_sample",
    )(x, cos, sin, g1a, g1b, gmix, wg, wu, wd, win)


def _banded_head_pair(q, k_prev, k_cur, v_prev, v_cur, has_prev, sinks):
    rows = q.shape[0]
    nblk = rows // QBLK
    keys = 2 * QBLK

    low_lanes = lax.broadcasted_iota(jnp.int32, (rows, LANES), 1) < HEAD_DIM
    zero = jnp.zeros_like(q)
    q2 = jnp.concatenate([jnp.where(low_lanes, q, zero).reshape(nblk, QBLK, LANES),
                          jnp.where(low_lanes, zero, q).reshape(nblk, QBLK, LANES)], axis=1)
    k_all = jnp.concatenate([k_prev, k_cur], axis=0)
    v_all = jnp.concatenate([v_prev, v_cur], axis=0)
    k2 = jnp.stack([k_all[b * QBLK:b * QBLK + keys] for b in range(nblk)])
    ones = jnp.ones((keys, LANES), BF16)
    v2 = jnp.stack([jnp.concatenate([v_all[b * QBLK:b * QBLK + keys], ones], axis=1)
                    for b in range(nblk)])

    s = jnp.einsum("bqd,bkd->bqk", q2, k2, preferred_element_type=F32)
    qrow = lax.broadcasted_iota(jnp.int32, (keys, keys), 0) & (QBLK - 1)
    col = lax.broadcasted_iota(jnp.int32, (keys, keys), 1)
    in_band = jnp.logical_and(col >= qrow, col <= qrow + QBLK)
    s = jnp.where(in_band[None], s, -jnp.inf)
    first_ok = jnp.logical_or(has_prev, col >= QBLK)
    s = jnp.concatenate([jnp.where(first_ok[None], s[0:1], -jnp.inf), s[1:]], axis=0)
    m = jnp.max(s, axis=2, keepdims=True)
    if sinks is not None:
        head1 = lax.broadcasted_iota(jnp.int32, (keys, 1), 0) >= QBLK
        sink = jnp.where(head1, sinks[1], sinks[0])[None]
        m = jnp.maximum(m, sink)
    e = jnp.exp(s - m).astype(BF16)
    pv = jnp.einsum("bqk,bkd->bqd", e, v2, preferred_element_type=F32)
    acc, den = pv[:, :, :LANES], pv[:, :, LANES:]
    if sinks is not None:
        den = den + jnp.exp(sink - m)
    out2 = acc / den
    low3 = low_lanes.reshape(nblk, QBLK, LANES)
    out = jnp.where(low3, out2[:, :QBLK], out2[:, QBLK:]).reshape(rows, LANES)
    lse2 = m + jnp.log(den)
    lse = jnp.where(low3, lse2[:, :QBLK], lse2[:, QBLK:]).reshape(rows, LANES)
    return out, lse


def _banded_attention_kernel(*refs, with_sink, with_lse, shared_kv):
    refs = list(refs)
    sink_ref = refs.pop(0) if with_sink else None
    q_ref, kp_ref, kc_ref, vp_ref, vc_ref, o_ref = refs[:6]
    lse_ref = refs[6] if with_lse else None
    pairs = q_ref.shape[1] // LANES
    has_prev = pl.program_id(2) > 0
    for g in range(pairs):
        lanes = slice(g * LANES, (g + 1) * LANES)
        kg = g // shared_kv
        kv_lanes = slice(kg * LANES, (kg + 1) * LANES)
        sinks = None
        if with_sink:
            pair = pl.program_id(1) * pairs + g
            sinks = (sink_ref[2 * pair], sink_ref[2 * pair + 1])
        out, lse = _banded_head_pair(q_ref[:, lanes], kp_ref[:, kv_lanes], kc_ref[:, kv_lanes],
                                     vp_ref[:, kv_lanes], vc_ref[:, kv_lanes], has_prev, sinks)
        o_ref[:, lanes] = out.astype(o_ref.dtype)
        if with_lse:
            lse_ref[:, lanes] = lse


def _banded_attention(q, k, v, *, shared_kv=1, sinks=None, with_lse, name):
    nb, length, width = q.shape
    rows = min(ATT_ROWS, length)
    sub = rows // QBLK
    pairs = max(shared_kv, min(ATT_BLOCKS // sub, width // LANES))
    assert pairs % shared_kv == 0 and (width // LANES) % pairs == 0
    grid = (nb, width // (pairs * LANES), length // rows)
    kv_lanes = pairs * LANES // shared_kv
    q_spec = pl.BlockSpec((None, rows, pairs * LANES), lambda b, c, i: (b, i, c))
    cur_spec = pl.BlockSpec((None, rows, kv_lanes), lambda b, c, i: (b, i, c))
    prev_spec = pl.BlockSpec((None, QBLK, kv_lanes),
                             lambda b, c, i: (b, jnp.maximum(i * sub - 1, 0), c))
    in_specs = [q_spec, prev_spec, cur_spec, prev_spec, cur_spec]
    args = [q, k, k, v, v]
    if sinks is not None:
        in_specs = [pl.BlockSpec(memory_space=pltpu.SMEM)] + in_specs
        args = [sinks] + args
    out_shape = [jax.ShapeDtypeStruct(q.shape, BF16)]
    out_specs = [q_spec]
    if with_lse:
        out_shape.append(jax.ShapeDtypeStruct(q.shape, F32))
        out_specs.append(q_spec)
    return pl.pallas_call(
        functools.partial(_banded_attention_kernel, with_sink=sinks is not None, with_lse=with_lse,
                          shared_kv=shared_kv),
        out_shape=out_shape,
        grid=grid,
        in_specs=in_specs,
        out_specs=out_specs,
        compiler_params=pltpu.CompilerParams(
            dimension_semantics=("arbitrary",) * 3, vmem_limit_bytes=VMEM_LIMIT),
        name=name,
    )(*args)


def _stage3_stages(st, rows, mixed, h_ref, p_ref, norm_refs, weight_refs, out_ref):
    gmix_ref, g2a_ref, g2b_ref, gpa_ref, gpb_ref = norm_refs
    wout_ref, wg_ref, wu_ref, wd_ref, wpg_ref, wpp_ref = weight_refs

    def load():
        st["oa"], st["ob"] = mixed()

    def out_proj():
        st["y"] = (_mm(st.pop("oa"), wout_ref[0:QA_W, :])
                   + _mm(st.pop("ob"), wout_ref[QA_W:QA_W + QB_W, :]))

    def mix_residual():
        st["h"] = h_ref[rows, :] + _rmsnorm(st.pop("y"), gmix_ref[...])

    def ple_pre():
        st["u"] = _rmsnorm(st["h"], gpa_ref[...]).astype(BF16)
        st["p"] = p_ref[rows, :].astype(BF16)

    def ple_dots():
        st["gate"], st["proj"] = _mm(st.pop("u"), wpg_ref[...]), _mm(st.pop("p"), wpp_ref[...])

    def ple_residual():
        y = jax.nn.sigmoid(st.pop("gate")) * st.pop("proj")
        out_ref[rows, :] = st.pop("h") + _rmsnorm(y, gpb_ref[...])

    ffn = _swiglu_stages(st, g2a_ref, g2b_ref, wg_ref, wu_ref, wd_ref)
    head = _merge_stages([load, out_proj, mix_residual], ffn)
    return _merge_stages(head, [ple_pre, ple_dots, ple_residual])


def _stage3_prompt_kernel(sink_ref, cv_ref, cols_ref, w_ref, cbk_ref, cbv_ref,
                          h_ref, o1_ref, l1_ref, o4_ref, l4_ref, o16_ref, l16_ref, ob_ref, p_ref,
                          *rest):
    norm_refs, weight_refs = rest[:5], rest[5:11]
    sav_ref, so_ref, sbk_ref, sbv_ref, out_ref = rest[11:16]
    o4s_ref, l4s_ref, o16s_ref, l16s_ref = rest[16:]
    tm = h_ref.shape[0]
    bm = min(ROW_BLOCK, tm)

    def mixer(j):
        rows = slice(j * bm, (j + 1) * bm)

        def mixed():
            for d, src, dst in ((4, o4_ref, o4s_ref), (4, l4_ref, l4s_ref),
                                (16, o16_ref, o16s_ref), (16, l16_ref, l16s_ref)):
                n = bm // d
                for r in range(d):
                    for c in range(QA_W // LANES):
                        lo = r * QA_W + c * LANES
                        piece = src[j * n:(j + 1) * n, lo:lo + LANES].astype(F32)
                        dst[j, c, pl.ds(r, n, stride=d), :] = piece
            pieces = []
            for c in range(QA_W // LANES):
                cols = slice(c * LANES, (c + 1) * LANES)
                l1, l4, l16 = l1_ref[rows, cols], l4s_ref[j, c], l16s_ref[j, c]
                m = jnp.maximum(l1, jnp.maximum(l4, l16))
                w1, w4, w16 = jnp.exp(l1 - m), jnp.exp(l4 - m), jnp.exp(l16 - m)
                mix = w1 * o1_ref[rows, cols].astype(F32) + w4 * o4s_ref[j, c] + w16 * o16s_ref[j, c]
                pieces.append(mix / (w1 + w4 + w16))
            return jnp.concatenate(pieces, axis=1).astype(BF16), ob_ref[rows, :]

        return rows, mixed

    pipelines = []
    for j in range(tm // bm):
        rows, mixed = mixer(j)
        pipelines.append(_stage3_stages({}, rows, mixed, h_ref, p_ref, norm_refs, weight_refs, out_ref))
    pieces = _value_pass_pieces(sink_ref, cols_ref, w_ref, cv_ref, cbk_ref, cbv_ref,
                                so_ref, sav_ref, sbk_ref, sbv_ref)
    _run_staggered(pipelines, {0: pieces})


def _stage3_sample_kernel(h_ref, o_ref, p_ref, *rest):
    norm_refs, weight_refs, out_ref = rest[:5], rest[5:11], rest[11]
    mixed = lambda: (o_ref[:, 0:QA_W].astype(BF16), o_ref[:, QA_W:QA_W + QB_W].astype(BF16))
    _run_staggered([_stage3_stages({}, slice(None), mixed, h_ref, p_ref, norm_refs, weight_refs,
                                   out_ref)])


def _stage3_weight_specs():
    return [_whole((1, D_MODEL))] * 5 + [
        _whole((QA_W + QB_W, D_MODEL)), _whole((D_MODEL, D_FF)), _whole((D_MODEL, D_FF)),
        _whole((D_FF, D_MODEL)), _whole((D_MODEL, D_MODEL)), _whole((D_PLE, D_MODEL))]


def _stage3_prompt(sinks, h, o1, l1, o4, l4, o16, l16, ob, p, norms, weights,
                   cols_v, w, cav_t, cbk_t, cbv_t):
    nb, seq, _ = h.shape
    tm = ROW_TILE
    nt = seq // tm
    n, heads, _, la = cav_t.shape
    lb = cbk_t.shape[-1]
    assert nb * nt == n

    def rows(width, d=1):
        return pl.BlockSpec((None, tm // d, d * width), lambda b, i: (b, i, 0))

    in_specs = ([pl.BlockSpec(memory_space=pltpu.SMEM),
                 _per_sequence_spec(nt, heads, HEAD_DIM, la),
                 _per_sequence_spec(nt, HEAD_DIM, V_PASS_COLS),
                 _per_sequence_spec(nt, heads, la + STATS),
                 _per_sequence_spec(nt, N_KV_B, HEAD_DIM, lb),
                 _per_sequence_spec(nt, N_KV_B, HEAD_DIM, lb),
                 rows(D_MODEL), rows(QA_W), rows(QA_W), rows(QA_W, 4), rows(QA_W, 4),
                 rows(QA_W, 16), rows(QA_W, 16), rows(QB_W), rows(D_PLE)] + _stage3_weight_specs())
    out_shape = [jax.ShapeDtypeStruct(cav_t.shape, F32),
                 jax.ShapeDtypeStruct((n, HEAD_DIM, OUT_COLS), F32),
                 jax.ShapeDtypeStruct(cbk_t.shape, F32), jax.ShapeDtypeStruct(cbv_t.shape, F32),
                 jax.ShapeDtypeStruct(h.shape, F32)]
    out_specs = [_per_sequence_spec(nt, heads, HEAD_DIM, la),
                 _per_sequence_spec(nt, HEAD_DIM, OUT_COLS),
                 _per_sequence_spec(nt, N_KV_B, HEAD_DIM, lb),
                 _per_sequence_spec(nt, N_KV_B, HEAD_DIM, lb), rows(D_MODEL)]
    return pl.pallas_call(
        _stage3_prompt_kernel,
        out_shape=out_shape,
        grid=(nb, nt),
        in_specs=in_specs,
        out_specs=out_specs,
        scratch_shapes=[pltpu.VMEM((tm // ROW_BLOCK, QA_W // LANES, ROW_BLOCK, LANES), F32)] * 4,
        compiler_params=pltpu.CompilerParams(
            dimension_semantics=("arbitrary", "arbitrary"), vmem_limit_bytes=VMEM_LIMIT),
        name="stage3_prompt",
    )(sinks, cav_t, cols_v, w, cbk_t, cbv_t, h, o1, l1, o4, l4, o16, l16, ob, p, *norms, *weights)


def _stage3_sample(h, o, p, norms, weights):
    n = h.shape[0]
    return pl.pallas_call(
        _stage3_sample_kernel,
        out_shape=jax.ShapeDtypeStruct(h.shape, F32),
        grid=(1,),
        in_specs=[_whole((n, D_MODEL)), _whole((n, QA_W + QB_W)), _whole((n, D_PLE))]
        + _stage3_weight_specs(),
        out_specs=pl.BlockSpec((n, D_MODEL), lambda i: (0, 0)),
        compiler_params=pltpu.CompilerParams(
            dimension_semantics=("arbitrary",), vmem_limit_bytes=VMEM_LIMIT),
        name="stage3_sample",
    )(h, o, p, *norms, *weights)


def _rope_tables(pos):
    inv = jnp.power(ROPE_THETA, -jnp.arange(HALF, dtype=F32) * 2.0 / HEAD_DIM)
    ang = pos.astype(F32)[:, None] * inv[None, :]
    c, s = jnp.cos(ang), jnp.sin(ang)
    return jnp.concatenate([c, c, c, c], axis=-1), jnp.concatenate([-s, s, -s, s], axis=-1)


def _layer(i, hp, hs, caches, p_prompt, p_sample, norms, weights, sinks):
    (g1a, g1b, gmix_a, gmix_b, g2a, g2b, gpa, gpb) = [g[i][None, :] for g in norms]
    (wg1, wu1, wd1, win, wout, wg2, wu2, wd2, wpg, wpp) = [w[i].astype(BF16) for w in weights]
    rows_last = lambda z: z.transpose(0, 2, 3, 1)
    rows_first = lambda z: z.transpose(0, 3, 1, 2)
    cak_t, cav_t, cbk_t, cbv_t = [rows_last(c[i]) for c in caches]
    nb, seq, _ = hp.shape
    n_dec, dec_seq, _ = hs.shape
    sink = sinks[i].astype(F32)

    cos_p, sin_p = _rope_tables(jnp.arange(seq, dtype=jnp.int32))
    pos_s = jnp.broadcast_to(PAST_LEN + jnp.arange(dec_seq, dtype=jnp.int32)[None, :],
                             (n_dec, dec_seq)).reshape(-1)
    cos_s, sin_s = _rope_tables(pos_s)
    stage1_w = (g1a, g1b, gmix_a, wg1, wu1, wd1, win)
    stage3_n = (gmix_b, g2a, g2b, gpa, gpb)
    stage3_w = (wout, wg2, wu2, wd2, wpg, wpp)

    xs = hs.reshape(n_dec * dec_seq, D_MODEL)
    h1s, qa_s, ka_s, va_s, qb_s, kb_s, vb_s = _stage1_sample(xs, cos_s, sin_s, *stage1_w)
    cols_k = _columns([qa_s, ka_s])
    cols_v = _columns([va_s, qb_s, kb_s, vb_s])

    (nak_s, w_s, h1, qa1, ka1, va1, qa4, ka4, va4, qa16, ka16, va16, qb, kb2, vb2,
     nak_p, nav_p, nbk_p, nbv_p) = _stage1_prompt(hp, cos_p, sin_p, *stage1_w, cols_k, cak_t)
    o1, l1 = _banded_attention(qa1, ka1, va1, with_lse=True, name="dilated_d1")
    o4, l4 = _banded_attention(qa4, ka4, va4, with_lse=True, name="dilated_d4")
    o16, l16 = _banded_attention(qa16, ka16, va16, with_lse=True, name="dilated_d16")
    (ob,) = _banded_attention(qb, kb2, vb2, shared_kv=GROUP_B // 2, sinks=sink,
                              with_lse=False, name="swa_sink")
    nav_s, o_cols, nbk_s, nbv_s, hp = _stage3_prompt(
        sink, h1, o1, l1, o4, l4, o16, l16, ob, p_prompt[i], stage3_n, stage3_w,
        cols_v, w_s, cav_t, cbk_t, cbv_t)

    o_s = o_cols.transpose(0, 2, 1).reshape(n_dec, QA_W + QB_W)
    hs = _stage3_sample(h1s, o_s, p_sample[i].reshape(n_dec * dec_seq, D_PLE), stage3_n, stage3_w)
    hs = hs.reshape(n_dec, dec_seq, D_MODEL)

    heads5 = lambda z, h: z.reshape(z.shape[0], z.shape[1], h, HEAD_DIM)
    new = (heads5(nak_p, N_HEADS_A), heads5(nav_p, N_HEADS_A), heads5(nbk_p, N_KV_B),
           heads5(nbv_p, N_KV_B), rows_first(nak_s), rows_first(nav_s), rows_first(nbk_s),
           rows_first(nbv_s))
    return hp, hs, new


def kernel(x_prompt, x_sample, cache_a_k, cache_a_v, cache_b_k, cache_b_v, p_prompt, p_sample,
           norm_f1_pre, norm_f1_post, w_f1_gate, w_f1_up, w_f1_down,
           norm_mix_pre, norm_mix_post, w_in, sinks_b, w_out,
           norm_f2_pre, norm_f2_post, w_f2_gate, w_f2_up, w_f2_down,
           norm_ple_pre, norm_ple_post, w_ple_gate, w_ple_proj):
    assert x_sample.shape[1] == 1 and cache_a_k.shape[2] == WIN_A and cache_b_k.shape[2] == WIN_B
    norms = (norm_f1_pre, norm_f1_post, norm_mix_pre, norm_mix_post,
             norm_f2_pre, norm_f2_post, norm_ple_pre, norm_ple_post)
    weights = (w_f1_gate, w_f1_up, w_f1_down, w_in, w_out,
               w_f2_gate, w_f2_up, w_f2_down, w_ple_gate, w_ple_proj)
    caches = (cache_a_k, cache_a_v, cache_b_k, cache_b_v)
    hp, hs = x_prompt, x_sample
    per_layer = []
    for i in range(norm_f1_pre.shape[0]):
        hp, hs, new = _layer(i, hp, hs, caches, p_prompt, p_sample, norms, weights, sinks_b)
        per_layer.append(new)
    stacked = [jnp.stack([layer[j] for layer in per_layer]) for j in range(8)]
    return (hp, hs, *stacked)
```

```python
import functools

import jax
import jax.numpy as jnp
from jax import lax
from jax.experimental import pallas as pl
from jax.experimental.pallas import tpu as pltpu

F32 = jnp.float32
BF16 = jnp.bfloat16

D_MODEL = 1024
HEAD_DIM = 64
HALF = HEAD_DIM // 2
N_HEADS_A = 8
N_HEADS_B = 8
N_KV_B = 2
GROUP_B = N_HEADS_B // N_KV_B
DILATIONS = (1, 4, 16)
WIN_STEPS = 128
WIN_A = 2048
WIN_B = 128
PAST_LEN = 16384
D_FF = 2816
D_PLE = 256
ROPE_THETA = 10000.0
EPS = 1e-6
SCALE = HEAD_DIM ** -0.5
QA_W = N_HEADS_A * HEAD_DIM
QB_W = N_HEADS_B * HEAD_DIM
KVB_W = N_KV_B * HEAD_DIM
IN_W = 3 * QA_W + QB_W + 2 * KVB_W
LANES = 128
QBLK = 128

ROW_TILE = 128
ROW_BLOCK = 128
ATT_ROWS = 1024
ATT_BLOCKS = 32
VMEM_LIMIT = 56 * 1024 * 1024

K_PASS_COLS = 2 * N_HEADS_A
V_COL_QB, V_COL_KB, V_COL_VB = N_HEADS_A, N_HEADS_A + N_HEADS_B, N_HEADS_A + N_HEADS_B + N_KV_B
V_PASS_COLS = V_COL_VB + N_KV_B
OUT_COLS = N_HEADS_A + N_HEADS_B
STATS = LANES


def _whole(shape):
    nd = len(shape)
    return pl.BlockSpec(shape, lambda *_: (0,) * nd, pipeline_mode=pl.Buffered(1))


def _rmsnorm(x, g):
    return x * lax.rsqrt(jnp.mean(x * x, axis=-1, keepdims=True) + EPS) * g


def _mm(a, w):
    return jnp.dot(a, w, preferred_element_type=F32)


def _run_staggered(pipelines, extras=None):
    extras = extras or {}
    depth = len(pipelines[0])
    for t in range(depth + len(pipelines) - 1):
        for j, stages in enumerate(pipelines):
            if 0 <= t - j < depth:
                stages[t - j]()
        for piece in extras.get(t, ()):
            piece()


def _swiglu_stages(st, g_pre_ref, g_post_ref, wg_ref, wu_ref, wd_ref):
    def pre():
        st["u"] = _rmsnorm(st["h"], g_pre_ref[...]).astype(BF16)

    def gate_up():
        u = st.pop("u")
        st["gate"], st["up"] = _mm(u, wg_ref[...]), _mm(u, wu_ref[...])

    def activate():
        gate = st.pop("gate")
        st["act"] = (gate * jax.nn.sigmoid(gate) * st.pop("up")).astype(BF16)

    def down():
        st["y"] = _mm(st.pop("act"), wd_ref[...])

    def post():
        st["h"] = st["h"] + 0.5 * _rmsnorm(st.pop("y"), g_post_ref[...])

    return [pre, gate_up, activate, down, post]


def _merge_stages(first, second):
    last, head = first[-1], second[0]

    def both():
        last()
        head()

    return first[:-1] + [both] + second[1:]


def _rope(z, cos, sin_signed):
    rows, width = z.shape
    lane = lax.broadcasted_iota(jnp.int32, (rows, LANES), 1)
    first_half = (lane & HALF) == 0
    outs = []
    for c in range(width // LANES):
        zc = z[:, c * LANES:(c + 1) * LANES]
        partner = jnp.where(first_half,
                            pltpu.roll(zc, LANES - HALF, 1),
                            pltpu.roll(zc, HALF, 1))
        outs.append(zc * cos + partner * sin_signed)
    return outs[0] if len(outs) == 1 else jnp.concatenate(outs, axis=1)


_IN_WIDTHS = (QA_W, QA_W, QA_W, QB_W, KVB_W, KVB_W)


def _project_stages(st, rows, g_pre_ref, win_ref, cos_ref, sin_ref, emit):
    def pre():
        st["u"] = _rmsnorm(st["h"], g_pre_ref[...]).astype(BF16)

    def project():
        u, lo, pieces = st.pop("u"), 0, []
        for width in _IN_WIDTHS:
            pieces.append(_mm(u, win_ref[:, lo:lo + width]))
            lo += width
        st["z"] = pieces

    def rotate():
        qa, ka, va, qb, kb, vb = st.pop("z")
        cos, sin = cos_ref[rows, :], sin_ref[rows, :]
        emit(_rope(qa, cos, sin) * SCALE, _rope(ka, cos, sin), va,
             _rope(qb, cos, sin) * SCALE, _rope(kb, cos, sin), vb)

    return [pre, project, rotate]


def _stage1_stages(st, rows, x_ref, cos_ref, sin_ref, g1a_ref, g1b_ref, gmix_ref,
                   wg_ref, wu_ref, wd_ref, win_ref, h_ref, emit):
    def load():
        st["h"] = x_ref[rows, :]

    def store():
        h_ref[rows, :] = st["h"]

    ffn = _swiglu_stages(st, g1a_ref, g1b_ref, wg_ref, wu_ref, wd_ref)
    proj = _project_stages(st, rows, gmix_ref, win_ref, cos_ref, sin_ref, emit)
    return _merge_stages(_merge_stages([load], ffn), _merge_stages([store], proj))


def _dup_heads(x):
    lane = lax.broadcasted_iota(jnp.int32, x.shape, 1)
    swapped = pltpu.roll(x, HEAD_DIM, 1)
    lo = lane < HEAD_DIM
    return jnp.where(lo, x, swapped), jnp.where(lo, swapped, x)


def _slide_window(buf_t, new_col):
    length = buf_t.shape[1]
    lane = lax.broadcasted_iota(jnp.int32, buf_t.shape, 1)
    return jnp.where(lane == length - 1, new_col, pltpu.roll(buf_t, length - 1, 1))


def _key_pass_pieces(cols_ref, ck_ref, w_ref, nak_ref):
    heads, _, la = ck_ref.shape
    scores, new_scores = {}, {}

    def score(h):
        def run():
            q = cols_ref[:, h:h + 1]
            k_new = cols_ref[:, heads + h:heads + h + 1]
            scores[h] = jnp.sum(q * ck_ref[h], axis=0, keepdims=True)
            new_scores[h] = jnp.sum(q * k_new, axis=0, keepdims=True)

        return run

    def softmax():
        dist = la - lax.broadcasted_iota(jnp.int32, (1, la), 1)
        count = jnp.zeros((1, la), F32)
        for d in DILATIONS:
            in_pattern = jnp.logical_and((dist & (d - 1)) == 0, dist <= d * WIN_STEPS)
            count = count + in_pattern.astype(F32)
        s = jnp.concatenate([scores.pop(h) for h in range(heads)], axis=0)
        s_new = jnp.concatenate([new_scores.pop(h) for h in range(heads)], axis=0)
        s = jnp.where(count > 0.0, s, -jnp.inf)
        m = jnp.maximum(jnp.max(s, axis=1, keepdims=True), s_new)
        e = jnp.exp(s - m) * count
        e_new = float(len(DILATIONS)) * jnp.exp(s_new - m)
        den = jnp.sum(e, axis=1, keepdims=True) + e_new
        w_ref[:, 0:la] = e
        lane = lax.broadcasted_iota(jnp.int32, (heads, STATS), 1)
        w_ref[:, la:la + STATS] = jnp.where(lane < STATS // 2, e_new, den)

    def slide(h):
        def run():
            nak_ref[h] = _slide_window(ck_ref[h], cols_ref[:, heads + h:heads + h + 1])

        return run

    return [score(h) for h in range(heads)] + [softmax] + [slide(h) for h in range(heads)]


def _value_pass_pieces(sink_ref, cols_ref, w_ref, cv_ref, cbk_ref, cbv_ref,
                       out_ref, nav_ref, nbk_ref, nbv_ref):
    heads, _, la = cv_ref.shape
    partial = {}

    def accumulate(h):
        def run():
            weighted = cv_ref[h] * w_ref[h:h + 1, 0:la]
            acc = weighted[:, 0:LANES]
            for c in range(1, la // LANES):
                acc = acc + weighted[:, c * LANES:(c + 1) * LANES]
            partial[h] = acc

        return run

    def finalize():
        acc = jnp.stack([partial.pop(h) for h in range(heads)])
        e_new = jnp.stack([w_ref[h:h + 1, la:la + 1] for h in range(heads)])
        half = la + STATS // 2
        den = jnp.stack([w_ref[h:h + 1, half:half + 1] for h in range(heads)])
        v_new = jnp.stack([cols_ref[:, h:h + 1] for h in range(heads)])
        out = (jnp.sum(acc, axis=2, keepdims=True) + e_new * v_new) / den
        for h in range(heads):
            out_ref[:, h:h + 1] = out[h]

    def slide(h):
        def run():
            nav_ref[h] = _slide_window(cv_ref[h], cols_ref[:, h:h + 1])

        return run

    def group_b():
        gidx = lax.broadcasted_iota(jnp.int32, (GROUP_B, 1, 1), 0)
        for kv in range(N_KV_B):
            k_t, v_t = cbk_ref[kv], cbv_ref[kv]
            k_new = cols_ref[:, V_COL_KB + kv:V_COL_KB + kv + 1]
            v_new = cols_ref[:, V_COL_VB + kv:V_COL_VB + kv + 1]
            first = V_COL_QB + kv * GROUP_B
            q = jnp.stack([cols_ref[:, first + g:first + g + 1] for g in range(GROUP_B)])
            sink = jnp.zeros((GROUP_B, 1, 1), F32)
            for g in range(GROUP_B):
                sink = jnp.where(gidx == g, sink_ref[kv * GROUP_B + g], sink)
            s = jnp.sum(q * k_t[None], axis=1, keepdims=True)
            s_new = jnp.sum(q * k_new[None], axis=1, keepdims=True)
            m = jnp.maximum(jnp.maximum(jnp.max(s, axis=2, keepdims=True), s_new), sink)
            e = jnp.exp(s - m)
            e_new = jnp.exp(s_new - m)
            den = jnp.sum(e, axis=2, keepdims=True) + e_new + jnp.exp(sink - m)
            out = (jnp.sum(v_t[None] * e, axis=2, keepdims=True) + e_new * v_new[None]) / den
            for g in range(GROUP_B):
                lane = N_HEADS_A + kv * GROUP_B + g
                out_ref[:, lane:lane + 1] = out[g]
            nbk_ref[kv] = _slide_window(k_t, k_new)
            nbv_ref[kv] = _slide_window(v_t, v_new)

    return ([group_b] + [accumulate(h) for h in range(heads)] + [finalize]
            + [slide(h) for h in range(heads)])


def _columns(pieces):
    n = pieces[0].shape[0]
    return jnp.concatenate(pieces, axis=1).reshape(n, -1, HEAD_DIM).transpose(0, 2, 1)


def _stage1_prompt_kernel(x_ref, cos_ref, sin_ref, g1a_ref, g1b_ref, gmix_ref,
                          wg_ref, wu_ref, wd_ref, win_ref, cols_ref, ck_ref,
                          h_ref, qa1_ref, ka1_ref, va1_ref, qa4_ref, ka4_ref, va4_ref,
                          qa16_ref, ka16_ref, va16_ref, qb_ref, kb2_ref, vb2_ref,
                          nak_ref, nav_ref, nbk_ref, nbv_ref, w_ref, sak_ref, stage_ref):
    tm = x_ref.shape[0]
    bm = min(ROW_BLOCK, tm)
    nblocks = tm // bm

    def emitter(j):
        rows = slice(j * bm, (j + 1) * bm)

        def decimated(z, nat_ref, dec_refs):
            nat_ref[rows, :] = z.astype(BF16)
            for c in range(QA_W // LANES):
                stage_ref[j, c] = z[:, c * LANES:(c + 1) * LANES]
            for d, ref in dec_refs:
                n = bm // d
                for r in range(d):
                    for c in range(QA_W // LANES):
                        lo = r * QA_W + c * LANES
                        piece = stage_ref[j, c, pl.ds(r, n, stride=d), :]
                        ref[j * n:(j + 1) * n, lo:lo + LANES] = piece.astype(BF16)

        def emit(qa, ka, va, qb, kb, vb):
            decimated(qa, qa1_ref, ((4, qa4_ref), (16, qa16_ref)))
            decimated(ka, ka1_ref, ((4, ka4_ref), (16, ka16_ref)))
            decimated(va, va1_ref, ((4, va4_ref), (16, va16_ref)))
            qb_ref[rows, :] = qb.astype(BF16)
            k0, k1 = _dup_heads(kb)
            kb2_ref[rows, 0:LANES] = k0.astype(BF16)
            kb2_ref[rows, LANES:2 * LANES] = k1.astype(BF16)
            v0, v1 = _dup_heads(vb)
            vb2_ref[rows, 0:LANES] = v0.astype(BF16)
            vb2_ref[rows, LANES:2 * LANES] = v1.astype(BF16)
            nak_ref[rows, :] = ka
            nav_ref[rows, :] = va
            if j == nblocks - 1:
                nbk_ref[...] = kb[bm - WIN_B:, :]
                nbv_ref[...] = vb[bm - WIN_B:, :]

        return rows, emit

    pipelines = []
    for j in range(nblocks):
        rows, emit = emitter(j)
        pipelines.append(_stage1_stages({}, rows, x_ref, cos_ref, sin_ref, g1a_ref, g1b_ref, gmix_ref,
                                        wg_ref, wu_ref, wd_ref, win_ref, h_ref, emit))
    pieces = _key_pass_pieces(cols_ref, ck_ref, w_ref, sak_ref)
    _run_staggered(pipelines, {0: pieces})


def _stage1_sample_kernel(x_ref, cos_ref, sin_ref, g1a_ref, g1b_ref, gmix_ref,
                          wg_ref, wu_ref, wd_ref, win_ref,
                          h_ref, qa_ref, ka_ref, va_ref, qb_ref, kb_ref, vb_ref):
    def emit(*pieces):
        for ref, z in zip((qa_ref, ka_ref, va_ref, qb_ref, kb_ref, vb_ref), pieces):
            ref[...] = z

    _run_staggered([_stage1_stages({}, slice(None), x_ref, cos_ref, sin_ref, g1a_ref, g1b_ref,
                                   gmix_ref, wg_ref, wu_ref, wd_ref, win_ref, h_ref, emit)])


def _stage1_weight_specs():
    return [_whole((1, D_MODEL))] * 3 + [
        _whole((D_MODEL, D_FF)), _whole((D_MODEL, D_FF)), _whole((D_FF, D_MODEL)),
        _whole((D_MODEL, IN_W))]


def _per_sequence_spec(tiles_per_batch, *block):
    return pl.BlockSpec((None,) + block,
                        lambda b, i: (b * tiles_per_batch + i,) + (0,) * len(block))


def _stage1_prompt(x, cos, sin, g1a, g1b, gmix, wg, wu, wd, win, cols_k, cak_t):
    nb, seq, _ = x.shape
    tm = ROW_TILE
    nt = seq // tm
    n, heads, _, la = cak_t.shape
    assert nb * nt == n and tm >= WIN_B
    first_kept = (seq - WIN_A) // tm

    def rows(width, dtype, d=1):
        shape = jax.ShapeDtypeStruct((nb, seq // d, d * width), dtype)
        spec = pl.BlockSpec((None, tm // d, d * width), lambda b, i: (b, i, 0))
        return shape, spec

    outs = [rows(D_MODEL, F32)]
    outs += [rows(QA_W, BF16)] * 3 + [rows(QA_W, BF16, 4)] * 3 + [rows(QA_W, BF16, 16)] * 3
    outs += [rows(QB_W, BF16), rows(2 * LANES, BF16), rows(2 * LANES, BF16)]
    kept = (jax.ShapeDtypeStruct((nb, WIN_A, QA_W), F32),
            pl.BlockSpec((None, tm, QA_W), lambda b, i: (b, jnp.maximum(i - first_kept, 0), 0)))
    last = (jax.ShapeDtypeStruct((nb, WIN_B, KVB_W), F32),
            pl.BlockSpec((None, WIN_B, KVB_W), lambda b, i: (b, 0, 0)))
    outs += [kept, kept, last, last]
    outs += [(jax.ShapeDtypeStruct((n, heads, la + STATS), F32), _per_sequence_spec(nt, heads, la + STATS)),
             (jax.ShapeDtypeStruct(cak_t.shape, F32), _per_sequence_spec(nt, heads, HEAD_DIM, la))]
    in_specs = ([pl.BlockSpec((None, tm, D_MODEL), lambda b, i: (b, i, 0)),
                 pl.BlockSpec((tm, LANES), lambda b, i: (i, 0)),
                 pl.BlockSpec((tm, LANES), lambda b, i: (i, 0))] + _stage1_weight_specs()
                + [_per_sequence_spec(nt, HEAD_DIM, K_PASS_COLS),
                   _per_sequence_spec(nt, heads, HEAD_DIM, la)])
    return pl.pallas_call(
        _stage1_prompt_kernel,
        out_shape=[o[0] for o in outs],
        grid=(nb, nt),
        in_specs=in_specs,
        out_specs=[o[1] for o in outs],
        scratch_shapes=[pltpu.VMEM((tm // ROW_BLOCK, QA_W // LANES, ROW_BLOCK, LANES), F32)],
        compiler_params=pltpu.CompilerParams(
            dimension_semantics=("arbitrary", "arbitrary"), vmem_limit_bytes=VMEM_LIMIT),
        name="stage1_prompt",
    )(x, cos, sin, g1a, g1b, gmix, wg, wu, wd, win, cols_k, cak_t)


def _stage1_sample(x, cos, sin, g1a, g1b, gmix, wg, wu, wd, win):
    n = x.shape[0]
    widths = (D_MODEL, QA_W, QA_W, QA_W, QB_W, KVB_W, KVB_W)
    return pl.pallas_call(
        _stage1_sample_kernel,
        out_shape=[jax.ShapeDtypeStruct((n, w), F32) for w in widths],
        grid=(1,),
        in_specs=[_whole((n, D_MODEL)), _whole((n, LANES)), _whole((n, LANES))]
        + _stage1_weight_specs(),
        out_specs=[pl.BlockSpec((n, w), lambda i: (0, 0)) for w in widths],
        compiler_params=pltpu.CompilerParams(
            dimension_semantics=("arbitrary",), vmem_limit_bytes=VMEM_LIMIT),
        name="stage1_sample",
    )(x, cos, sin, g1a, g1b, gmix, wg, wu, wd, win)


def _banded_head_pair(q, k_prev, k_cur, v_prev, v_cur, has_prev, sinks):
    rows = q.shape[0]
    nblk = rows // QBLK
    keys = 2 * QBLK

    low_lanes = lax.broadcasted_iota(jnp.int32, (rows, LANES), 1) < HEAD_DIM
    zero = jnp.zeros_like(q)
    q2 = jnp.concatenate([jnp.where(low_lanes, q, zero).reshape(nblk, QBLK, LANES),
                          jnp.where(low_lanes, zero, q).reshape(nblk, QBLK, LANES)], axis=1)
    k_all = jnp.concatenate([k_prev, k_cur], axis=0)
    v_all = jnp.concatenate([v_prev, v_cur], axis=0)
    k2 = jnp.stack([k_all[b * QBLK:b * QBLK + keys] for b in range(nblk)])
    ones = jnp.ones((keys, LANES), BF16)
    v2 = jnp.stack([jnp.concatenate([v_all[b * QBLK:b * QBLK + keys], ones], axis=1)
                    for b in range(nblk)])

    s = jnp.einsum("bqd,bkd->bqk", q2, k2, preferred_element_type=F32)
    qrow = lax.broadcasted_iota(jnp.int32, (keys, keys), 0) & (QBLK - 1)
    col = lax.broadcasted_iota(jnp.int32, (keys, keys), 1)
    in_band = jnp.logical_and(col >= qrow, col <= qrow + QBLK)
    s = jnp.where(in_band[None], s, -jnp.inf)
    first_ok = jnp.logical_or(has_prev, col >= QBLK)
    s = jnp.concatenate([jnp.where(first_ok[None], s[0:1], -jnp.inf), s[1:]], axis=0)
    m = jnp.max(s, axis=2, keepdims=True)
    if sinks is not None:
        head1 = lax.broadcasted_iota(jnp.int32, (keys, 1), 0) >= QBLK
        sink = jnp.where(head1, sinks[1], sinks[0])[None]
        m = jnp.maximum(m, sink)
    e = jnp.exp(s - m).astype(BF16)
    pv = jnp.einsum("bqk,bkd->bqd", e, v2, preferred_element_type=F32)
    acc, den = pv[:, :, :LANES], pv[:, :, LANES:]
    if sinks is not None:
        den = den + jnp.exp(sink - m)
    out2 = acc / den
    low3 = low_lanes.reshape(nblk, QBLK, LANES)
    out = jnp.where(low3, out2[:, :QBLK], out2[:, QBLK:]).reshape(rows, LANES)
    lse2 = m + jnp.log(den)
    lse = jnp.where(low3, lse2[:, :QBLK], lse2[:, QBLK:]).reshape(rows, LANES)
    return out, lse


def _banded_attention_kernel(*refs, with_sink, with_lse, shared_kv):
    refs = list(refs)
    sink_ref = refs.pop(0) if with_sink else None
    q_ref, kp_ref, kc_ref, vp_ref, vc_ref, o_ref = refs[:6]
    lse_ref = refs[6] if with_lse else None
    pairs = q_ref.shape[1] // LANES
    has_prev = pl.program_id(2) > 0
    for g in range(pairs):
        lanes = slice(g * LANES, (g + 1) * LANES)
        kg = g // shared_kv
        kv_lanes = slice(kg * LANES, (kg + 1) * LANES)
        sinks = None
        if with_sink:
            pair = pl.program_id(1) * pairs + g
            sinks = (sink_ref[2 * pair], sink_ref[2 * pair + 1])
        out, lse = _banded_head_pair(q_ref[:, lanes], kp_ref[:, kv_lanes], kc_ref[:, kv_lanes],
                                     vp_ref[:, kv_lanes], vc_ref[:, kv_lanes], has_prev, sinks)
        o_ref[:, lanes] = out.astype(o_ref.dtype)
        if with_lse:
            lse_ref[:, lanes] = lse


def _banded_attention(q, k, v, *, shared_kv=1, sinks=None, with_lse, name):
    nb, length, width = q.shape
    rows = min(ATT_ROWS, length)
    sub = rows // QBLK
    pairs = max(shared_kv, min(ATT_BLOCKS // sub, width // LANES))
    assert pairs % shared_kv == 0 and (width // LANES) % pairs == 0
    grid = (nb, width // (pairs * LANES), length // rows)
    kv_lanes = pairs * LANES // shared_kv
    q_spec = pl.BlockSpec((None, rows, pairs * LANES), lambda b, c, i: (b, i, c))
    cur_spec = pl.BlockSpec((None, rows, kv_lanes), lambda b, c, i: (b, i, c))
    prev_spec = pl.BlockSpec((None, QBLK, kv_lanes),
                             lambda b, c, i: (b, jnp.maximum(i * sub - 1, 0), c))
    in_specs = [q_spec, prev_spec, cur_spec, prev_spec, cur_spec]
    args = [q, k, k, v, v]
    if sinks is not None:
        in_specs = [pl.BlockSpec(memory_space=pltpu.SMEM)] + in_specs
        args = [sinks] + args
    out_shape = [jax.ShapeDtypeStruct(q.shape, BF16)]
    out_specs = [q_spec]
    if with_lse:
        out_shape.append(jax.ShapeDtypeStruct(q.shape, F32))
        out_specs.append(q_spec)
    return pl.pallas_call(
        functools.partial(_banded_attention_kernel, with_sink=sinks is not None, with_lse=with_lse,
                          shared_kv=shared_kv),
        out_shape=out_shape,
        grid=grid,
        in_specs=in_specs,
        out_specs=out_specs,
        compiler_params=pltpu.CompilerParams(
            dimension_semantics=("arbitrary",) * 3, vmem_limit_bytes=VMEM_LIMIT),
        name=name,
    )(*args)


def _stage3_stages(st, rows, mixed, h_ref, p_ref, norm_refs, weight_refs, out_ref):
    gmix_ref, g2a_ref, g2b_ref, gpa_ref, gpb_ref = norm_refs
    wout_ref, wg_ref, wu_ref, wd_ref, wpg_ref, wpp_ref = weight_refs

    def load():
        st["oa"], st["ob"] = mixed()

    def out_proj():
        st["y"] = (_mm(st.pop("oa"), wout_ref[0:QA_W, :])
                   + _mm(st.pop("ob"), wout_ref[QA_W:QA_W + QB_W, :]))

    def mix_residual():
        st["h"] = h_ref[rows, :] + _rmsnorm(st.pop("y"), gmix_ref[...])

    def ple_pre():
        st["u"] = _rmsnorm(st["h"], gpa_ref[...]).astype(BF16)
        st["p"] = p_ref[rows, :].astype(BF16)

    def ple_dots():
        st["gate"], st["proj"] = _mm(st.pop("u"), wpg_ref[...]), _mm(st.pop("p"), wpp_ref[...])

    def ple_residual():
        y = jax.nn.sigmoid(st.pop("gate")) * st.pop("proj")
        out_ref[rows, :] = st.pop("h") + _rmsnorm(y, gpb_ref[...])

    ffn = _swiglu_stages(st, g2a_ref, g2b_ref, wg_ref, wu_ref, wd_ref)
    head = _merge_stages([load, out_proj, mix_residual], ffn)
    return _merge_stages(head, [ple_pre, ple_dots, ple_residual])


def _stage3_prompt_kernel(sink_ref, h_ref, o1_ref, l1_ref, o4_ref, l4_ref, o16_ref, l16_ref,
                          ob_ref, p_ref, *rest):
    norm_refs, weight_refs = rest[:5], rest[5:11]
    cols_ref, w_ref, cv_ref, cbk_ref, cbv_ref = rest[11:16]
    out_ref, so_ref, sav_ref, sbk_ref, sbv_ref = rest[16:21]
    o4s_ref, l4s_ref, o16s_ref, l16s_ref = rest[21:]
    tm = h_ref.shape[0]
    bm = min(ROW_BLOCK, tm)

    def mixer(j):
        rows = slice(j * bm, (j + 1) * bm)

        def mixed():
            for d, src, dst in ((4, o4_ref, o4s_ref), (4, l4_ref, l4s_ref),
                                (16, o16_ref, o16s_ref), (16, l16_ref, l16s_ref)):
                n = bm // d
                for r in range(d):
                    for c in range(QA_W // LANES):
                        lo = r * QA_W + c * LANES
                        piece = src[j * n:(j + 1) * n, lo:lo + LANES].astype(F32)
                        dst[j, c, pl.ds(r, n, stride=d), :] = piece
            pieces = []
            for c in range(QA_W // LANES):
                cols = slice(c * LANES, (c + 1) * LANES)
                l1, l4, l16 = l1_ref[rows, cols], l4s_ref[j, c], l16s_ref[j, c]
                m = jnp.maximum(l1, jnp.maximum(l4, l16))
                w1, w4, w16 = jnp.exp(l1 - m), jnp.exp(l4 - m), jnp.exp(l16 - m)
                mix = w1 * o1_ref[rows, cols].astype(F32) + w4 * o4s_ref[j, c] + w16 * o16s_ref[j, c]
                pieces.append(mix / (w1 + w4 + w16))
            return jnp.concatenate(pieces, axis=1).astype(BF16), ob_ref[rows, :]

        return rows, mixed

    pipelines = []
    for j in range(tm // bm):
        rows, mixed = mixer(j)
        pipelines.append(_stage3_stages({}, rows, mixed, h_ref, p_ref, norm_refs, weight_refs, out_ref))
    pieces = _value_pass_pieces(sink_ref, cols_ref, w_ref, cv_ref, cbk_ref, cbv_ref,
                                so_ref, sav_ref, sbk_ref, sbv_ref)
    _run_staggered(pipelines, {0: pieces})


def _stage3_sample_kernel(h_ref, o_ref, p_ref, *rest):
    norm_refs, weight_refs, out_ref = rest[:5], rest[5:11], rest[11]
    mixed = lambda: (o_ref[:, 0:QA_W].astype(BF16), o_ref[:, QA_W:QA_W + QB_W].astype(BF16))
    _run_staggered([_stage3_stages({}, slice(None), mixed, h_ref, p_ref, norm_refs, weight_refs,
                                   out_ref)])


def _stage3_weight_specs():
    return [_whole((1, D_MODEL))] * 5 + [
        _whole((QA_W + QB_W, D_MODEL)), _whole((D_MODEL, D_FF)), _whole((D_MODEL, D_FF)),
        _whole((D_FF, D_MODEL)), _whole((D_MODEL, D_MODEL)), _whole((D_PLE, D_MODEL))]


def _stage3_prompt(sinks, h, o1, l1, o4, l4, o16, l16, ob, p, norms, weights,
                   cols_v, w, cav_t, cbk_t, cbv_t):
    nb, seq, _ = h.shape
    tm = ROW_TILE
    nt = seq // tm
    n, heads, _, la = cav_t.shape
    lb = cbk_t.shape[-1]
    assert nb * nt == n

    def rows(width, d=1):
        return pl.BlockSpec((None, tm // d, d * width), lambda b, i: (b, i, 0))

    in_specs = ([pl.BlockSpec(memory_space=pltpu.SMEM),
                 rows(D_MODEL), rows(QA_W), rows(QA_W), rows(QA_W, 4), rows(QA_W, 4),
                 rows(QA_W, 16), rows(QA_W, 16), rows(QB_W), rows(D_PLE)] + _stage3_weight_specs()
                + [_per_sequence_spec(nt, HEAD_DIM, V_PASS_COLS),
                   _per_sequence_spec(nt, heads, la + STATS),
                   _per_sequence_spec(nt, heads, HEAD_DIM, la),
                   _per_sequence_spec(nt, N_KV_B, HEAD_DIM, lb),
                   _per_sequence_spec(nt, N_KV_B, HEAD_DIM, lb)])
    out_shape = [jax.ShapeDtypeStruct(h.shape, F32),
                 jax.ShapeDtypeStruct((n, HEAD_DIM, OUT_COLS), F32),
                 jax.ShapeDtypeStruct(cav_t.shape, F32),
                 jax.ShapeDtypeStruct(cbk_t.shape, F32), jax.ShapeDtypeStruct(cbv_t.shape, F32)]
    out_specs = [rows(D_MODEL), _per_sequence_spec(nt, HEAD_DIM, OUT_COLS),
                 _per_sequence_spec(nt, heads, HEAD_DIM, la),
                 _per_sequence_spec(nt, N_KV_B, HEAD_DIM, lb),
                 _per_sequence_spec(nt, N_KV_B, HEAD_DIM, lb)]
    return pl.pallas_call(
        _stage3_prompt_kernel,
        out_shape=out_shape,
        grid=(nb, nt),
        in_specs=in_specs,
        out_specs=out_specs,
        scratch_shapes=[pltpu.VMEM((tm // ROW_BLOCK, QA_W // LANES, ROW_BLOCK, LANES), F32)] * 4,
        compiler_params=pltpu.CompilerParams(
            dimension_semantics=("arbitrary", "arbitrary"), vmem_limit_bytes=VMEM_LIMIT),
        name="stage3_prompt",
    )(sinks, h, o1, l1, o4, l4, o16, l16, ob, p, *norms, *weights, cols_v, w, cav_t, cbk_t, cbv_t)


def _stage3_sample(h, o, p, norms, weights):
    n = h.shape[0]
    return pl.pallas_call(
        _stage3_sample_kernel,
        out_shape=jax.ShapeDtypeStruct(h.shape, F32),
        grid=(1,),
        in_specs=[_whole((n, D_MODEL)), _whole((n, QA_W + QB_W)), _whole((n, D_PLE))]
        + _stage3_weight_specs(),
        out_specs=pl.BlockSpec((n, D_MODEL), lambda i: (0, 0)),
        compiler_params=pltpu.CompilerParams(
            dimension_semantics=("arbitrary",), vmem_limit_bytes=VMEM_LIMIT),
        name="stage3_sample",
    )(h, o, p, *norms, *weights)


def _rope_tables(pos):
    inv = jnp.power(ROPE_THETA, -jnp.arange(HALF, dtype=F32) * 2.0 / HEAD_DIM)
    inv = jnp.tile(inv, LANES // HALF)
    sign = jnp.where((jnp.arange(LANES) & HALF) == 0, -1.0, 1.0).astype(F32)
    ang = pos.astype(F32)[:, None] * inv[None, :]
    return jnp.cos(ang), jnp.sin(ang) * sign[None, :]


def _layer(i, hp, hs, caches, p_prompt, p_sample, norms, weights, sinks):
    (g1a, g1b, gmix_a, gmix_b, g2a, g2b, gpa, gpb) = [g[i][None, :] for g in norms]
    (wg1, wu1, wd1, win, wout, wg2, wu2, wd2, wpg, wpp) = [w[i].astype(BF16) for w in weights]
    rows_last = lambda z: z.transpose(0, 2, 3, 1)
    rows_first = lambda z: z.transpose(0, 3, 1, 2)
    cak_t, cav_t, cbk_t, cbv_t = [rows_last(c[i]) for c in caches]
    nb, seq, _ = hp.shape
    n_dec, dec_seq, _ = hs.shape
    sink = sinks[i].astype(F32)

    cos_p, sin_p = _rope_tables(jnp.arange(seq, dtype=jnp.int32))
    pos_s = jnp.broadcast_to(PAST_LEN + jnp.arange(dec_seq, dtype=jnp.int32)[None, :],
                             (n_dec, dec_seq)).reshape(-1)
    cos_s, sin_s = _rope_tables(pos_s)
    stage1_w = (g1a, g1b, gmix_a, wg1, wu1, wd1, win)
    stage3_n = (gmix_b, g2a, g2b, gpa, gpb)
    stage3_w = (wout, wg2, wu2, wd2, wpg, wpp)

    xs = hs.reshape(n_dec * dec_seq, D_MODEL)
    h1s, qa_s, ka_s, va_s, qb_s, kb_s, vb_s = _stage1_sample(xs, cos_s, sin_s, *stage1_w)
    cols_k = _columns([qa_s, ka_s])
    cols_v = _columns([va_s, qb_s, kb_s, vb_s])

    (h1, qa1, ka1, va1, qa4, ka4, va4, qa16, ka16, va16, qb, kb2, vb2,
     nak_p, nav_p, nbk_p, nbv_p, w_s, nak_s) = _stage1_prompt(hp, cos_p, sin_p, *stage1_w,
                                                             cols_k, cak_t)
    o1, l1 = _banded_attention(qa1, ka1, va1, with_lse=True, name="dilated_d1")
    o4, l4 = _banded_attention(qa4, ka4, va4, with_lse=True, name="dilated_d4")
    o16, l16 = _banded_attention(qa16, ka16, va16, with_lse=True, name="dilated_d16")
    (ob,) = _banded_attention(qb, kb2, vb2, shared_kv=GROUP_B // 2, sinks=sink,
                              with_lse=False, name="swa_sink")
    hp, o_cols, nav_s, nbk_s, nbv_s = _stage3_prompt(
        sink, h1, o1, l1, o4, l4, o16, l16, ob, p_prompt[i], stage3_n, stage3_w,
        cols_v, w_s, cav_t, cbk_t, cbv_t)

    o_s = o_cols.transpose(0, 2, 1).reshape(n_dec, QA_W + QB_W)
    hs = _stage3_sample(h1s, o_s, p_sample[i].reshape(n_dec * dec_seq, D_PLE), stage3_n, stage3_w)
    hs = hs.reshape(n_dec, dec_seq, D_MODEL)

    heads5 = lambda z, h: z.reshape(z.shape[0], z.shape[1], h, HEAD_DIM)
    new = (heads5(nak_p, N_HEADS_A), heads5(nav_p, N_HEADS_A), heads5(nbk_p, N_KV_B),
           heads5(nbv_p, N_KV_B), rows_first(nak_s), rows_first(nav_s), rows_first(nbk_s),
           rows_first(nbv_s))
    return hp, hs, new


def kernel(x_prompt, x_sample, cache_a_k, cache_a_v, cache_b_k, cache_b_v, p_prompt, p_sample,
           norm_f1_pre, norm_f1_post, w_f1_gate, w_f1_up, w_f1_down,
           norm_mix_pre, norm_mix_post, w_in, sinks_b, w_out,
           norm_f2_pre, norm_f2_post, w_f2_gate, w_f2_up, w_f2_down,
           norm_ple_pre, norm_ple_post, w_ple_gate, w_ple_proj):
    assert x_sample.shape[1] == 1 and cache_a_k.shape[2] == WIN_A and cache_b_k.shape[2] == WIN_B
    norms = (norm_f1_pre, norm_f1_post, norm_mix_pre, norm_mix_post,
             norm_f2_pre, norm_f2_post, norm_ple_pre, norm_ple_post)
    weights = (w_f1_gate, w_f1_up, w_f1_down, w_in, w_out,
               w_f2_gate, w_f2_up, w_f2_down, w_ple_gate, w_ple_proj)
    caches = (cache_a_k, cache_a_v, cache_b_k, cache_b_v)
    hp, hs = x_prompt, x_sample
    per_layer = []
    for i in range(norm_f1_pre.shape[0]):
        hp, hs, new = _layer(i, hp, hs, caches, p_prompt, p_sample, norms, weights, sinks_b)
        per_layer.append(new)
    stacked = [jnp.stack([layer[j] for layer in per_layer]) for j in range(8)]
    return (hp, hs, *stacked)
```

```python
import functools

import jax
import jax.numpy as jnp
from jax import lax
from jax.experimental import pallas as pl
from jax.experimental.pallas import tpu as pltpu

F32 = jnp.float32
BF16 = jnp.bfloat16

D_MODEL = 1024
HEAD_DIM = 64
HALF = HEAD_DIM // 2
N_HEADS_A = 8
N_HEADS_B = 8
N_KV_B = 2
GROUP_B = N_HEADS_B // N_KV_B
DILATIONS = (1, 4, 16)
WIN_STEPS = 128
WIN_A = 2048
WIN_B = 128
PAST_LEN = 16384
D_FF = 2816
D_PLE = 256
ROPE_THETA = 10000.0
EPS = 1e-6
SCALE = HEAD_DIM ** -0.5
QA_W = N_HEADS_A * HEAD_DIM
QB_W = N_HEADS_B * HEAD_DIM
KVB_W = N_KV_B * HEAD_DIM
IN_W = 3 * QA_W + QB_W + 2 * KVB_W
LANES = 128
QBLK = 128

ROW_TILE = 128
ROW_BLOCK = 128
ATT_ROWS = 1024
ATT_BLOCKS = 32
ROW_CHUNK = 512
VMEM_LIMIT = 56 * 1024 * 1024

K_PASS_COLS = 2 * N_HEADS_A
V_COL_QB, V_COL_KB, V_COL_VB = N_HEADS_A, N_HEADS_A + N_HEADS_B, N_HEADS_A + N_HEADS_B + N_KV_B
V_PASS_COLS = V_COL_VB + N_KV_B
OUT_COLS = N_HEADS_A + N_HEADS_B
STATS = LANES


def _whole(shape):
    nd = len(shape)
    return pl.BlockSpec(shape, lambda *_: (0,) * nd, pipeline_mode=pl.Buffered(1))


def _rmsnorm(x, g):
    return x * lax.rsqrt(jnp.mean(x * x, axis=-1, keepdims=True) + EPS) * g


def _mm(a, w):
    return jnp.dot(a, w, preferred_element_type=F32)


def _run_staggered(pipelines, extras=None):
    extras = extras or {}
    depth = len(pipelines[0])
    for t in range(depth + len(pipelines) - 1):
        for j, stages in enumerate(pipelines):
            if 0 <= t - j < depth:
                stages[t - j]()
        for piece in extras.get(t, ()):
            piece()


def _swiglu_stages(st, g_pre_ref, g_post_ref, wg_ref, wu_ref, wd_ref):
    def pre():
        st["u"] = _rmsnorm(st["h"], g_pre_ref[...]).astype(BF16)

    def gate_up():
        u = st.pop("u")
        st["gate"], st["up"] = _mm(u, wg_ref[...]), _mm(u, wu_ref[...])

    def activate():
        gate = st.pop("gate")
        st["act"] = (gate * jax.nn.sigmoid(gate) * st.pop("up")).astype(BF16)

    def down():
        st["y"] = _mm(st.pop("act"), wd_ref[...])

    def post():
        st["h"] = st["h"] + 0.5 * _rmsnorm(st.pop("y"), g_post_ref[...])

    return [pre, gate_up, activate, down, post]


def _merge_stages(first, second):
    last, head = first[-1], second[0]

    def both():
        last()
        head()

    return first[:-1] + [both] + second[1:]


def _rope(z, cos, sin_signed):
    rows, width = z.shape
    lane = lax.broadcasted_iota(jnp.int32, (rows, LANES), 1)
    first_half = (lane & HALF) == 0
    outs = []
    for c in range(width // LANES):
        zc = z[:, c * LANES:(c + 1) * LANES]
        partner = jnp.where(first_half,
                            pltpu.roll(zc, LANES - HALF, 1),
                            pltpu.roll(zc, HALF, 1))
        outs.append(zc * cos + partner * sin_signed)
    return outs[0] if len(outs) == 1 else jnp.concatenate(outs, axis=1)


_IN_WIDTHS = (QA_W, QA_W, QA_W, QB_W, KVB_W, KVB_W)


def _project_stages(st, rows, g_pre_ref, win_ref, cos_ref, sin_ref, emit):
    def pre():
        st["u"] = _rmsnorm(st["h"], g_pre_ref[...]).astype(BF16)

    def project():
        u, lo, pieces = st.pop("u"), 0, []
        for width in _IN_WIDTHS:
            pieces.append(_mm(u, win_ref[:, lo:lo + width]))
            lo += width
        st["z"] = pieces

    def rotate():
        qa, ka, va, qb, kb, vb = st.pop("z")
        cos, sin = cos_ref[rows, :], sin_ref[rows, :]
        emit(_rope(qa, cos, sin) * SCALE, _rope(ka, cos, sin), va,
             _rope(qb, cos, sin) * SCALE, _rope(kb, cos, sin), vb)

    return [pre, project, rotate]


def _stage1_stages(st, rows, x_ref, cos_ref, sin_ref, g1a_ref, g1b_ref, gmix_ref,
                   wg_ref, wu_ref, wd_ref, win_ref, h_ref, emit):
    def load():
        st["h"] = x_ref[rows, :]

    def store():
        h_ref[rows, :] = st["h"]

    ffn = _swiglu_stages(st, g1a_ref, g1b_ref, wg_ref, wu_ref, wd_ref)
    proj = _project_stages(st, rows, gmix_ref, win_ref, cos_ref, sin_ref, emit)
    return _merge_stages(_merge_stages([load], ffn), _merge_stages([store], proj))


def _dup_heads(x):
    lane = lax.broadcasted_iota(jnp.int32, x.shape, 1)
    swapped = pltpu.roll(x, HEAD_DIM, 1)
    lo = lane < HEAD_DIM
    return jnp.where(lo, x, swapped), jnp.where(lo, swapped, x)


def _slide_window(buf_t, new_col):
    length = buf_t.shape[1]
    lane = lax.broadcasted_iota(jnp.int32, buf_t.shape, 1)
    return jnp.where(lane == length - 1, new_col, pltpu.roll(buf_t, length - 1, 1))


def _key_pass_pieces(cols_ref, ck_ref, w_ref, nak_ref):
    heads, _, la = ck_ref.shape
    scores, new_scores = {}, {}

    def score(h):
        def run():
            q = cols_ref[:, h:h + 1]
            k_new = cols_ref[:, heads + h:heads + h + 1]
            scores[h] = jnp.sum(q * ck_ref[h], axis=0, keepdims=True)
            new_scores[h] = jnp.sum(q * k_new, axis=0, keepdims=True)

        return run

    def softmax():
        dist = la - lax.broadcasted_iota(jnp.int32, (1, la), 1)
        count = jnp.zeros((1, la), F32)
        for d in DILATIONS:
            in_pattern = jnp.logical_and((dist & (d - 1)) == 0, dist <= d * WIN_STEPS)
            count = count + in_pattern.astype(F32)
        s = jnp.concatenate([scores.pop(h) for h in range(heads)], axis=0)
        s_new = jnp.concatenate([new_scores.pop(h) for h in range(heads)], axis=0)
        s = jnp.where(count > 0.0, s, -jnp.inf)
        m = jnp.maximum(jnp.max(s, axis=1, keepdims=True), s_new)
        e = jnp.exp(s - m) * count
        e_new = float(len(DILATIONS)) * jnp.exp(s_new - m)
        den = jnp.sum(e, axis=1, keepdims=True) + e_new
        w_ref[:, 0:la] = e
        lane = lax.broadcasted_iota(jnp.int32, (heads, STATS), 1)
        w_ref[:, la:la + STATS] = jnp.where(lane < STATS // 2, e_new, den)

    def slide(h):
        def run():
            nak_ref[h] = _slide_window(ck_ref[h], cols_ref[:, heads + h:heads + h + 1])

        return run

    return [score(h) for h in range(heads)] + [softmax] + [slide(h) for h in range(heads)]


def _value_pass_pieces(sink_ref, cols_ref, w_ref, cv_ref, cbk_ref, cbv_ref,
                       out_ref, nav_ref, nbk_ref, nbv_ref):
    heads, _, la = cv_ref.shape
    partial = {}

    def accumulate(h):
        def run():
            weighted = cv_ref[h] * w_ref[h:h + 1, 0:la]
            acc = weighted[:, 0:LANES]
            for c in range(1, la // LANES):
                acc = acc + weighted[:, c * LANES:(c + 1) * LANES]
            partial[h] = acc

        return run

    def finalize():
        acc = jnp.stack([partial.pop(h) for h in range(heads)])
        e_new = jnp.stack([w_ref[h:h + 1, la:la + 1] for h in range(heads)])
        half = la + STATS // 2
        den = jnp.stack([w_ref[h:h + 1, half:half + 1] for h in range(heads)])
        v_new = jnp.stack([cols_ref[:, h:h + 1] for h in range(heads)])
        out = (jnp.sum(acc, axis=2, keepdims=True) + e_new * v_new) / den
        for h in range(heads):
            out_ref[:, h:h + 1] = out[h]

    def slide(h):
        def run():
            nav_ref[h] = _slide_window(cv_ref[h], cols_ref[:, h:h + 1])

        return run

    def group_b():
        gidx = lax.broadcasted_iota(jnp.int32, (GROUP_B, 1, 1), 0)
        for kv in range(N_KV_B):
            k_t, v_t = cbk_ref[kv], cbv_ref[kv]
            k_new = cols_ref[:, V_COL_KB + kv:V_COL_KB + kv + 1]
            v_new = cols_ref[:, V_COL_VB + kv:V_COL_VB + kv + 1]
            first = V_COL_QB + kv * GROUP_B
            q = jnp.stack([cols_ref[:, first + g:first + g + 1] for g in range(GROUP_B)])
            sink = jnp.zeros((GROUP_B, 1, 1), F32)
            for g in range(GROUP_B):
                sink = jnp.where(gidx == g, sink_ref[kv * GROUP_B + g], sink)
            s = jnp.sum(q * k_t[None], axis=1, keepdims=True)
            s_new = jnp.sum(q * k_new[None], axis=1, keepdims=True)
            m = jnp.maximum(jnp.maximum(jnp.max(s, axis=2, keepdims=True), s_new), sink)
            e = jnp.exp(s - m)
            e_new = jnp.exp(s_new - m)
            den = jnp.sum(e, axis=2, keepdims=True) + e_new + jnp.exp(sink - m)
            out = (jnp.sum(v_t[None] * e, axis=2, keepdims=True) + e_new * v_new[None]) / den
            for g in range(GROUP_B):
                lane = N_HEADS_A + kv * GROUP_B + g
                out_ref[:, lane:lane + 1] = out[g]
            nbk_ref[kv] = _slide_window(k_t, k_new)
            nbv_ref[kv] = _slide_window(v_t, v_new)

    return ([group_b] + [accumulate(h) for h in range(heads)] + [finalize]
            + [slide(h) for h in range(heads)])


def _columns(pieces):
    n = pieces[0].shape[0]
    return jnp.concatenate(pieces, axis=1).reshape(n, -1, HEAD_DIM).transpose(0, 2, 1)


def _stage1_prompt_kernel(x_ref, cos_ref, sin_ref, g1a_ref, g1b_ref, gmix_ref,
                          wg_ref, wu_ref, wd_ref, win_ref, cols_ref, ck_ref,
                          h_ref, qa1_ref, ka1_ref, va1_ref, qa4_ref, ka4_ref, va4_ref,
                          qa16_ref, ka16_ref, va16_ref, qb_ref, kb2_ref, vb2_ref,
                          nak_ref, nav_ref, nbk_ref, nbv_ref, w_ref, sak_ref, stage_ref):
    tm = x_ref.shape[0]
    bm = min(ROW_BLOCK, tm)
    nblocks = tm // bm

    def emitter(j):
        rows = slice(j * bm, (j + 1) * bm)

        def decimated(z, nat_ref, dec_refs):
            nat_ref[rows, :] = z.astype(BF16)
            for c in range(QA_W // LANES):
                stage_ref[j, c] = z[:, c * LANES:(c + 1) * LANES]
            for d, ref in dec_refs:
                n = bm // d
                for r in range(d):
                    for c in range(QA_W // LANES):
                        lo = r * QA_W + c * LANES
                        piece = stage_ref[j, c, pl.ds(r, n, stride=d), :]
                        ref[j * n:(j + 1) * n, lo:lo + LANES] = piece.astype(BF16)

        def emit(qa, ka, va, qb, kb, vb):
            decimated(qa, qa1_ref, ((4, qa4_ref), (16, qa16_ref)))
            decimated(ka, ka1_ref, ((4, ka4_ref), (16, ka16_ref)))
            decimated(va, va1_ref, ((4, va4_ref), (16, va16_ref)))
            qb_ref[rows, :] = qb.astype(BF16)
            k0, k1 = _dup_heads(kb)
            kb2_ref[rows, 0:LANES] = k0.astype(BF16)
            kb2_ref[rows, LANES:2 * LANES] = k1.astype(BF16)
            v0, v1 = _dup_heads(vb)
            vb2_ref[rows, 0:LANES] = v0.astype(BF16)
            vb2_ref[rows, LANES:2 * LANES] = v1.astype(BF16)
            nak_ref[rows, :] = ka
            nav_ref[rows, :] = va
            if j == nblocks - 1:
                nbk_ref[...] = kb[bm - WIN_B:, :]
                nbv_ref[...] = vb[bm - WIN_B:, :]

        return rows, emit

    pipelines = []
    for j in range(nblocks):
        rows, emit = emitter(j)
        pipelines.append(_stage1_stages({}, rows, x_ref, cos_ref, sin_ref, g1a_ref, g1b_ref, gmix_ref,
                                        wg_ref, wu_ref, wd_ref, win_ref, h_ref, emit))
    pieces = _key_pass_pieces(cols_ref, ck_ref, w_ref, sak_ref)
    _run_staggered(pipelines, {0: pieces})


def _stage1_sample_kernel(x_ref, cos_ref, sin_ref, g1a_ref, g1b_ref, gmix_ref,
                          wg_ref, wu_ref, wd_ref, win_ref,
                          h_ref, qa_ref, ka_ref, va_ref, qb_ref, kb_ref, vb_ref):
    def emit(*pieces):
        for ref, z in zip((qa_ref, ka_ref, va_ref, qb_ref, kb_ref, vb_ref), pieces):
            ref[...] = z

    _run_staggered([_stage1_stages({}, slice(None), x_ref, cos_ref, sin_ref, g1a_ref, g1b_ref,
                                   gmix_ref, wg_ref, wu_ref, wd_ref, win_ref, h_ref, emit)])


def _stage1_weight_specs():
    return [_whole((1, D_MODEL))] * 3 + [
        _whole((D_MODEL, D_FF)), _whole((D_MODEL, D_FF)), _whole((D_FF, D_MODEL)),
        _whole((D_MODEL, IN_W))]


def _per_sequence_spec(tiles_per_batch, *block):
    return pl.BlockSpec((None,) + block,
                        lambda b, i: (b * tiles_per_batch + i,) + (0,) * len(block))


def _stage1_prompt(x, cos, sin, g1a, g1b, gmix, wg, wu, wd, win, cols_k, cak_t):
    nb, seq, _ = x.shape
    tm = ROW_TILE
    nt = seq // tm
    n, heads, _, la = cak_t.shape
    assert nb * nt == n and tm >= WIN_B
    first_kept = (seq - WIN_A) // tm

    def rows(width, dtype, d=1):
        shape = jax.ShapeDtypeStruct((nb, seq // d, d * width), dtype)
        spec = pl.BlockSpec((None, tm // d, d * width), lambda b, i: (b, i, 0))
        return shape, spec

    outs = [rows(D_MODEL, F32)]
    outs += [rows(QA_W, BF16)] * 3 + [rows(QA_W, BF16, 4)] * 3 + [rows(QA_W, BF16, 16)] * 3
    outs += [rows(QB_W, BF16), rows(2 * LANES, BF16), rows(2 * LANES, BF16)]
    kept = (jax.ShapeDtypeStruct((nb, WIN_A, QA_W), F32),
            pl.BlockSpec((None, tm, QA_W), lambda b, i: (b, jnp.maximum(i - first_kept, 0), 0)))
    last = (jax.ShapeDtypeStruct((nb, WIN_B, KVB_W), F32),
            pl.BlockSpec((None, WIN_B, KVB_W), lambda b, i: (b, 0, 0)))
    outs += [kept, kept, last, last]
    outs += [(jax.ShapeDtypeStruct((n, heads, la + STATS), F32), _per_sequence_spec(nt, heads, la + STATS)),
             (jax.ShapeDtypeStruct(cak_t.shape, F32), _per_sequence_spec(nt, heads, HEAD_DIM, la))]
    in_specs = ([pl.BlockSpec((None, tm, D_MODEL), lambda b, i: (b, i, 0)),
                 pl.BlockSpec((tm, LANES), lambda b, i: (i, 0)),
                 pl.BlockSpec((tm, LANES), lambda b, i: (i, 0))] + _stage1_weight_specs()
                + [_per_sequence_spec(nt, HEAD_DIM, K_PASS_COLS),
                   _per_sequence_spec(nt, heads, HEAD_DIM, la)])
    return pl.pallas_call(
        _stage1_prompt_kernel,
        out_shape=[o[0] for o in outs],
        grid=(nb, nt),
        in_specs=in_specs,
        out_specs=[o[1] for o in outs],
        scratch_shapes=[pltpu.VMEM((tm // ROW_BLOCK, QA_W // LANES, ROW_BLOCK, LANES), F32)],
        compiler_params=pltpu.CompilerParams(
            dimension_semantics=("arbitrary", "arbitrary"), vmem_limit_bytes=VMEM_LIMIT),
        name="stage1_prompt",
    )(x, cos, sin, g1a, g1b, gmix, wg, wu, wd, win, cols_k, cak_t)


def _stage1_sample(x, cos, sin, g1a, g1b, gmix, wg, wu, wd, win):
    n = x.shape[0]
    widths = (D_MODEL, QA_W, QA_W, QA_W, QB_W, KVB_W, KVB_W)
    return pl.pallas_call(
        _stage1_sample_kernel,
        out_shape=[jax.ShapeDtypeStruct((n, w), F32) for w in widths],
        grid=(1,),
        in_specs=[_whole((n, D_MODEL)), _whole((n, LANES)), _whole((n, LANES))]
        + _stage1_weight_specs(),
        out_specs=[pl.BlockSpec((n, w), lambda i: (0, 0)) for w in widths],
        compiler_params=pltpu.CompilerParams(
            dimension_semantics=("arbitrary",), vmem_limit_bytes=VMEM_LIMIT),
        name="stage1_sample",
    )(x, cos, sin, g1a, g1b, gmix, wg, wu, wd, win)


def _banded_head_pair(q, k_prev, k_cur, v_prev, v_cur, has_prev, sinks):
    rows = q.shape[0]
    nblk = rows // QBLK
    keys = 2 * QBLK

    low_lanes = lax.broadcasted_iota(jnp.int32, (rows, LANES), 1) < HEAD_DIM
    zero = jnp.zeros_like(q)
    q2 = jnp.concatenate([jnp.where(low_lanes, q, zero).reshape(nblk, QBLK, LANES),
                          jnp.where(low_lanes, zero, q).reshape(nblk, QBLK, LANES)], axis=1)
    k_all = jnp.concatenate([k_prev, k_cur], axis=0)
    v_all = jnp.concatenate([v_prev, v_cur], axis=0)
    k2 = jnp.stack([k_all[b * QBLK:b * QBLK + keys] for b in range(nblk)])
    ones = jnp.ones((keys, LANES), BF16)
    v2 = jnp.stack([jnp.concatenate([v_all[b * QBLK:b * QBLK + keys], ones], axis=1)
                    for b in range(nblk)])

    s = jnp.einsum("bqd,bkd->bqk", q2, k2, preferred_element_type=F32)
    qrow = lax.broadcasted_iota(jnp.int32, (keys, keys), 0) & (QBLK - 1)
    col = lax.broadcasted_iota(jnp.int32, (keys, keys), 1)
    in_band = jnp.logical_and(col >= qrow, col <= qrow + QBLK)
    s = jnp.where(in_band[None], s, -jnp.inf)
    first_ok = jnp.logical_or(has_prev, col >= QBLK)
    s = jnp.concatenate([jnp.where(first_ok[None], s[0:1], -jnp.inf), s[1:]], axis=0)
    m = jnp.max(s, axis=2, keepdims=True)
    if sinks is not None:
        head1 = lax.broadcasted_iota(jnp.int32, (keys, 1), 0) >= QBLK
        sink = jnp.where(head1, sinks[1], sinks[0])[None]
        m = jnp.maximum(m, sink)
    e = jnp.exp(s - m).astype(BF16)
    pv = jnp.einsum("bqk,bkd->bqd", e, v2, preferred_element_type=F32)
    acc, den = pv[:, :, :LANES], pv[:, :, LANES:]
    if sinks is not None:
        den = den + jnp.exp(sink - m)
    out2 = acc / den
    low3 = low_lanes.reshape(nblk, QBLK, LANES)
    out = jnp.where(low3, out2[:, :QBLK], out2[:, QBLK:]).reshape(rows, LANES)
    lse2 = m + jnp.log(den)
    lse = jnp.where(low3, lse2[:, :QBLK], lse2[:, QBLK:]).reshape(rows, LANES)
    return out, lse


def _banded_attention_kernel(*refs, with_sink, with_lse, shared_kv):
    refs = list(refs)
    sink_ref = refs.pop(0) if with_sink else None
    q_ref, kp_ref, kc_ref, vp_ref, vc_ref, o_ref = refs[:6]
    lse_ref = refs[6] if with_lse else None
    pairs = q_ref.shape[1] // LANES
    has_prev = pl.program_id(2) > 0
    for g in range(pairs):
        lanes = slice(g * LANES, (g + 1) * LANES)
        kg = g // shared_kv
        kv_lanes = slice(kg * LANES, (kg + 1) * LANES)
        sinks = None
        if with_sink:
            pair = pl.program_id(1) * pairs + g
            sinks = (sink_ref[2 * pair], sink_ref[2 * pair + 1])
        rows = q_ref.shape[0]
        chunk = min(ROW_CHUNK, rows)
        for lo in range(0, rows, chunk):
            cur = slice(lo, lo + chunk)
            if lo == 0:
                k_prev, v_prev, prev_ok = kp_ref[:, kv_lanes], vp_ref[:, kv_lanes], has_prev
            else:
                before = slice(lo - QBLK, lo)
                k_prev, v_prev, prev_ok = kc_ref[before, kv_lanes], vc_ref[before, kv_lanes], True
            out, lse = _banded_head_pair(q_ref[cur, lanes], k_prev, kc_ref[cur, kv_lanes],
                                         v_prev, vc_ref[cur, kv_lanes], prev_ok, sinks)
            o_ref[cur, lanes] = out.astype(o_ref.dtype)
            if with_lse:
                lse_ref[cur, lanes] = lse


def _banded_attention(q, k, v, *, shared_kv=1, sinks=None, with_lse, name):
    nb, length, width = q.shape
    rows = min(ATT_ROWS, length)
    sub = rows // QBLK
    pairs = max(shared_kv, min(ATT_BLOCKS // sub, width // LANES))
    assert pairs % shared_kv == 0 and (width // LANES) % pairs == 0
    grid = (nb, width // (pairs * LANES), length // rows)
    kv_lanes = pairs * LANES // shared_kv
    q_spec = pl.BlockSpec((None, rows, pairs * LANES), lambda b, c, i: (b, i, c))
    cur_spec = pl.BlockSpec((None, rows, kv_lanes), lambda b, c, i: (b, i, c))
    prev_spec = pl.BlockSpec((None, QBLK, kv_lanes),
                             lambda b, c, i: (b, jnp.maximum(i * sub - 1, 0), c))
    in_specs = [q_spec, prev_spec, cur_spec, prev_spec, cur_spec]
    args = [q, k, k, v, v]
    if sinks is not None:
        in_specs = [pl.BlockSpec(memory_space=pltpu.SMEM)] + in_specs
        args = [sinks] + args
    out_shape = [jax.ShapeDtypeStruct(q.shape, BF16)]
    out_specs = [q_spec]
    if with_lse:
        out_shape.append(jax.ShapeDtypeStruct(q.shape, F32))
        out_specs.append(q_spec)
    return pl.pallas_call(
        functools.partial(_banded_attention_kernel, with_sink=sinks is not None, with_lse=with_lse,
                          shared_kv=shared_kv),
        out_shape=out_shape,
        grid=grid,
        in_specs=in_specs,
        out_specs=out_specs,
        compiler_params=pltpu.CompilerParams(
            dimension_semantics=("arbitrary",) * 3, vmem_limit_bytes=VMEM_LIMIT),
        name=name,
    )(*args)


def _stage3_stages(st, rows, mixed, h_ref, p_ref, norm_refs, weight_refs, out_ref):
    gmix_ref, g2a_ref, g2b_ref, gpa_ref, gpb_ref = norm_refs
    wout_ref, wg_ref, wu_ref, wd_ref, wpg_ref, wpp_ref = weight_refs

    def load():
        st["oa"], st["ob"] = mixed()

    def out_proj():
        st["y"] = (_mm(st.pop("oa"), wout_ref[0:QA_W, :])
                   + _mm(st.pop("ob"), wout_ref[QA_W:QA_W + QB_W, :]))

    def mix_residual():
        st["h"] = h_ref[rows, :] + _rmsnorm(st.pop("y"), gmix_ref[...])

    def ple_pre():
        st["u"] = _rmsnorm(st["h"], gpa_ref[...]).astype(BF16)
        st["p"] = p_ref[rows, :].astype(BF16)

    def ple_dots():
        st["gate"], st["proj"] = _mm(st.pop("u"), wpg_ref[...]), _mm(st.pop("p"), wpp_ref[...])

    def ple_residual():
        y = jax.nn.sigmoid(st.pop("gate")) * st.pop("proj")
        out_ref[rows, :] = st.pop("h") + _rmsnorm(y, gpb_ref[...])

    ffn = _swiglu_stages(st, g2a_ref, g2b_ref, wg_ref, wu_ref, wd_ref)
    head = _merge_stages([load, out_proj, mix_residual], ffn)
    return _merge_stages(head, [ple_pre, ple_dots, ple_residual])


def _stage3_prompt_kernel(sink_ref, h_ref, o1_ref, l1_ref, o4_ref, l4_ref, o16_ref, l16_ref,
                          ob_ref, p_ref, *rest):
    norm_refs, weight_refs = rest[:5], rest[5:11]
    cols_ref, w_ref, cv_ref, cbk_ref, cbv_ref = rest[11:16]
    out_ref, so_ref, sav_ref, sbk_ref, sbv_ref = rest[16:21]
    o4s_ref, l4s_ref, o16s_ref, l16s_ref = rest[21:]
    tm = h_ref.shape[0]
    bm = min(ROW_BLOCK, tm)

    def mixer(j):
        rows = slice(j * bm, (j + 1) * bm)

        def mixed():
            for d, src, dst in ((4, o4_ref, o4s_ref), (4, l4_ref, l4s_ref),
                                (16, o16_ref, o16s_ref), (16, l16_ref, l16s_ref)):
                n = bm // d
                for r in range(d):
                    for c in range(QA_W // LANES):
                        lo = r * QA_W + c * LANES
                        piece = src[j * n:(j + 1) * n, lo:lo + LANES].astype(F32)
                        dst[j, c, pl.ds(r, n, stride=d), :] = piece
            pieces = []
            for c in range(QA_W // LANES):
                cols = slice(c * LANES, (c + 1) * LANES)
                l1, l4, l16 = l1_ref[rows, cols], l4s_ref[j, c], l16s_ref[j, c]
                m = jnp.maximum(l1, jnp.maximum(l4, l16))
                w1, w4, w16 = jnp.exp(l1 - m), jnp.exp(l4 - m), jnp.exp(l16 - m)
                mix = w1 * o1_ref[rows, cols].astype(F32) + w4 * o4s_ref[j, c] + w16 * o16s_ref[j, c]
                pieces.append(mix / (w1 + w4 + w16))
            return jnp.concatenate(pieces, axis=1).astype(BF16), ob_ref[rows, :]

        return rows, mixed

    pipelines = []
    for j in range(tm // bm):
        rows, mixed = mixer(j)
        pipelines.append(_stage3_stages({}, rows, mixed, h_ref, p_ref, norm_refs, weight_refs, out_ref))
    pieces = _value_pass_pieces(sink_ref, cols_ref, w_ref, cv_ref, cbk_ref, cbv_ref,
                                so_ref, sav_ref, sbk_ref, sbv_ref)
    _run_staggered(pipelines, {0: pieces})


def _stage3_sample_kernel(h_ref, o_ref, p_ref, *rest):
    norm_refs, weight_refs, out_ref = rest[:5], rest[5:11], rest[11]
    mixed = lambda: (o_ref[:, 0:QA_W].astype(BF16), o_ref[:, QA_W:QA_W + QB_W].astype(BF16))
    _run_staggered([_stage3_stages({}, slice(None), mixed, h_ref, p_ref, norm_refs, weight_refs,
                                   out_ref)])


def _stage3_weight_specs():
    return [_whole((1, D_MODEL))] * 5 + [
        _whole((QA_W + QB_W, D_MODEL)), _whole((D_MODEL, D_FF)), _whole((D_MODEL, D_FF)),
        _whole((D_FF, D_MODEL)), _whole((D_MODEL, D_MODEL)), _whole((D_PLE, D_MODEL))]


def _stage3_prompt(sinks, h, o1, l1, o4, l4, o16, l16, ob, p, norms, weights,
                   cols_v, w, cav_t, cbk_t, cbv_t):
    nb, seq, _ = h.shape
    tm = ROW_TILE
    nt = seq // tm
    n, heads, _, la = cav_t.shape
    lb = cbk_t.shape[-1]
    assert nb * nt == n

    def rows(width, d=1):
        return pl.BlockSpec((None, tm // d, d * width), lambda b, i: (b, i, 0))

    in_specs = ([pl.BlockSpec(memory_space=pltpu.SMEM),
                 rows(D_MODEL), rows(QA_W), rows(QA_W), rows(QA_W, 4), rows(QA_W, 4),
                 rows(QA_W, 16), rows(QA_W, 16), rows(QB_W), rows(D_PLE)] + _stage3_weight_specs()
                + [_per_sequence_spec(nt, HEAD_DIM, V_PASS_COLS),
                   _per_sequence_spec(nt, heads, la + STATS),
                   _per_sequence_spec(nt, heads, HEAD_DIM, la),
                   _per_sequence_spec(nt, N_KV_B, HEAD_DIM, lb),
                   _per_sequence_spec(nt, N_KV_B, HEAD_DIM, lb)])
    out_shape = [jax.ShapeDtypeStruct(h.shape, F32),
                 jax.ShapeDtypeStruct((n, HEAD_DIM, OUT_COLS), F32),
                 jax.ShapeDtypeStruct(cav_t.shape, F32),
                 jax.ShapeDtypeStruct(cbk_t.shape, F32), jax.ShapeDtypeStruct(cbv_t.shape, F32)]
    out_specs = [rows(D_MODEL), _per_sequence_spec(nt, HEAD_DIM, OUT_COLS),
                 _per_sequence_spec(nt, heads, HEAD_DIM, la),
                 _per_sequence_spec(nt, N_KV_B, HEAD_DIM, lb),
                 _per_sequence_spec(nt, N_KV_B, HEAD_DIM, lb)]
    return pl.pallas_call(
        _stage3_prompt_kernel,
        out_shape=out_shape,
        grid=(nb, nt),
        in_specs=in_specs,
        out_specs=out_specs,
        scratch_shapes=[pltpu.VMEM((tm // ROW_BLOCK, QA_W // LANES, ROW_BLOCK, LANES), F32)] * 4,
        compiler_params=pltpu.CompilerParams(
            dimension_semantics=("arbitrary", "arbitrary"), vmem_limit_bytes=VMEM_LIMIT),
        name="stage3_prompt",
    )(sinks, h, o1, l1, o4, l4, o16, l16, ob, p, *norms, *weights, cols_v, w, cav_t, cbk_t, cbv_t)


def _stage3_sample(h, o, p, norms, weights):
    n = h.shape[0]
    return pl.pallas_call(
        _stage3_sample_kernel,
        out_shape=jax.ShapeDtypeStruct(h.shape, F32),
        grid=(1,),
        in_specs=[_whole((n, D_MODEL)), _whole((n, QA_W + QB_W)), _whole((n, D_PLE))]
        + _stage3_weight_specs(),
        out_specs=pl.BlockSpec((n, D_MODEL), lambda i: (0, 0)),
        compiler_params=pltpu.CompilerParams(
            dimension_semantics=("arbitrary",), vmem_limit_bytes=VMEM_LIMIT),
        name="stage3_sample",
    )(h, o, p, *norms, *weights)


def _rope_tables(pos):
    inv = jnp.power(ROPE_THETA, -jnp.arange(HALF, dtype=F32) * 2.0 / HEAD_DIM)
    ang = pos.astype(F32)[:, None] * inv[None, :]
    c, s = jnp.cos(ang), jnp.sin(ang)
    return jnp.concatenate([c, c, c, c], axis=-1), jnp.concatenate([-s, s, -s, s], axis=-1)


def _layer(i, hp, hs, caches, p_prompt, p_sample, norms, weights, sinks):
    (g1a, g1b, gmix_a, gmix_b, g2a, g2b, gpa, gpb) = [g[i][None, :] for g in norms]
    (wg1, wu1, wd1, win, wout, wg2, wu2, wd2, wpg, wpp) = [w[i].astype(BF16) for w in weights]
    rows_last = lambda z: z.transpose(0, 2, 3, 1)
    rows_first = lambda z: z.transpose(0, 3, 1, 2)
    cak_t, cav_t, cbk_t, cbv_t = [rows_last(c[i]) for c in caches]
    nb, seq, _ = hp.shape
    n_dec, dec_seq, _ = hs.shape
    sink = sinks[i].astype(F32)

    cos_p, sin_p = _rope_tables(jnp.arange(seq, dtype=jnp.int32))
    pos_s = jnp.broadcast_to(PAST_LEN + jnp.arange(dec_seq, dtype=jnp.int32)[None, :],
                             (n_dec, dec_seq)).reshape(-1)
    cos_s, sin_s = _rope_tables(pos_s)
    stage1_w = (g1a, g1b, gmix_a, wg1, wu1, wd1, win)
    stage3_n = (gmix_b, g2a, g2b, gpa, gpb)
    stage3_w = (wout, wg2, wu2, wd2, wpg, wpp)

    xs = hs.reshape(n_dec * dec_seq, D_MODEL)
    h1s, qa_s, ka_s, va_s, qb_s, kb_s, vb_s = _stage1_sample(xs, cos_s, sin_s, *stage1_w)
    cols_k = _columns([qa_s, ka_s])
    cols_v = _columns([va_s, qb_s, kb_s, vb_s])

    (h1, qa1, ka1, va1, qa4, ka4, va4, qa16, ka16, va16, qb, kb2, vb2,
     nak_p, nav_p, nbk_p, nbv_p, w_s, nak_s) = _stage1_prompt(hp, cos_p, sin_p, *stage1_w,
                                                             cols_k, cak_t)
    o1, l1 = _banded_attention(qa1, ka1, va1, with_lse=True, name="dilated_d1")
    o4, l4 = _banded_attention(qa4, ka4, va4, with_lse=True, name="dilated_d4")
    o16, l16 = _banded_attention(qa16, ka16, va16, with_lse=True, name="dilated_d16")
    (ob,) = _banded_attention(qb, kb2, vb2, shared_kv=GROUP_B // 2, sinks=sink,
                              with_lse=False, name="swa_sink")
    hp, o_cols, nav_s, nbk_s, nbv_s = _stage3_prompt(
        sink, h1, o1, l1, o4, l4, o16, l16, ob, p_prompt[i], stage3_n, stage3_w,
        cols_v, w_s, cav_t, cbk_t, cbv_t)

    o_s = o_cols.transpose(0, 2, 1).reshape(n_dec, QA_W + QB_W)
    hs = _stage3_sample(h1s, o_s, p_sample[i].reshape(n_dec * dec_seq, D_PLE), stage3_n, stage3_w)
    hs = hs.reshape(n_dec, dec_seq, D_MODEL)

    heads5 = lambda z, h: z.reshape(z.shape[0], z.shape[1], h, HEAD_DIM)
    new = (heads5(nak_p, N_HEADS_A), heads5(nav_p, N_HEADS_A), heads5(nbk_p, N_KV_B),
           heads5(nbv_p, N_KV_B), rows_first(nak_s), rows_first(nav_s), rows_first(nbk_s),
           rows_first(nbv_s))
    return hp, hs, new


def kernel(x_prompt, x_sample, cache_a_k, cache_a_v, cache_b_k, cache_b_v, p_prompt, p_sample,
           norm_f1_pre, norm_f1_post, w_f1_gate, w_f1_up, w_f1_down,
           norm_mix_pre, norm_mix_post, w_in, sinks_b, w_out,
           norm_f2_pre, norm_f2_post, w_f2_gate, w_f2_up, w_f2_down,
           norm_ple_pre, norm_ple_post, w_ple_gate, w_ple_proj):
    assert x_sample.shape[1] == 1 and cache_a_k.shape[2] == WIN_A and cache_b_k.shape[2] == WIN_B
    norms = (norm_f1_pre, norm_f1_post, norm_mix_pre, norm_mix_post,
             norm_f2_pre, norm_f2_post, norm_ple_pre, norm_ple_post)
    weights = (w_f1_gate, w_f1_up, w_f1_down, w_in, w_out,
               w_f2_gate, w_f2_up, w_f2_down, w_ple_gate, w_ple_proj)
    caches = (cache_a_k, cache_a_v, cache_b_k, cache_b_v)
    hp, hs = x_prompt, x_sample
    per_layer = []
    for i in range(norm_f1_pre.shape[0]):
        hp, hs, new = _layer(i, hp, hs, caches, p_prompt, p_sample, norms, weights, sinks_b)
        per_layer.append(new)
    stacked = [jnp.stack([layer[j] for layer in per_layer]) for j in range(8)]
    return (hp, hs, *stacked)
```
